```python
import jax, jax.numpy as jnp
from jax import lax
import numpy as np

D_MODEL = 1024
BATCH = 8
SEQ = 8192
DEPTH = 2

N_MEM = 256
D_MIX = D_MODEL
POOL_WIDTH = D_MIX // 2
POOL_WINDOWS = (2, 4, 8, 16)
POOL_GROUPS = len(POOL_WINDOWS)
POOL_GROUP_DIM = POOL_WIDTH // POOL_GROUPS
SGU_WIDTH = D_MIX - POOL_WIDTH
SGU_HEADS = 4
SGU_HEAD_DIM = SGU_WIDTH // SGU_HEADS
CHUNK = 128
D_IN_PROJ = POOL_WIDTH + 2 * SGU_WIDTH
XATTN_HEADS = 4
XATTN_HEAD_DIM = D_MODEL // XATTN_HEADS
D_FF = 2816
CONV_WIDTH = 3
EPS = 1e-6

kernel_name = "hybrid_pool_sgu_memxattn_convffn"


def rmsnorm(x, g):
    xf = x.astype(jnp.float32)
    y = xf * lax.rsqrt(jnp.mean(xf * xf, axis=-1, keepdims=True) + EPS)
    return (y * g.astype(jnp.float32)).astype(x.dtype)


def layernorm_nobias(x, g):
    xf = x.astype(jnp.float32)
    mu = jnp.mean(xf, axis=-1, keepdims=True)
    xc = xf - mu
    y = xc * lax.rsqrt(jnp.mean(xc * xc, axis=-1, keepdims=True) + EPS)
    return (y * g.astype(jnp.float32)).astype(x.dtype)


def pool_mixer(p, pool_w, pool_scale):
    B, S, _ = p.shape
    pf = p.astype(jnp.float32)
    c = jnp.pad(jnp.cumsum(pf, axis=1), ((0, 0), (1, 0), (0, 0)))
    t = jnp.arange(S)
    diffs = []
    for gi, win in enumerate(POOL_WINDOWS):
        sl = slice(gi * POOL_GROUP_DIM, (gi + 1) * POOL_GROUP_DIM)
        cg = c[..., sl]
        prev = jnp.pad(cg, ((0, 0), (win - 1, 0), (0, 0)))[:, :S]
        count = jnp.minimum(t + 1, win).astype(jnp.float32)[None, :, None]
        diffs.append((cg[:, 1:] - prev) / count - pf[..., sl])
    d = jnp.stack(diffs, axis=2).astype(p.dtype)
    y = jnp.einsum('bsgc,gcd->bsgd', d, pool_w).reshape(B, S, POOL_WIDTH)
    return y * pool_scale


def sgu_mixer(u, v, sgu_g, sgu_w, sgu_b):
    B, S, _ = u.shape
    vn = layernorm_nobias(v, sgu_g)
    vc = vn.reshape(B, S // CHUNK, CHUNK, SGU_HEADS, SGU_HEAD_DIM)
    mask = jnp.tril(jnp.ones((CHUNK, CHUNK), dtype=bool))
    w_masked = jnp.where(mask[None], sgu_w, jnp.zeros_like(sgu_w))
    z = jnp.einsum('hts,bnshd->bnthd', w_masked, vc) + sgu_b.T[:, :, None]
    return u * z.reshape(B, S, SGU_WIDTH)


def mem_cross_attention(xn, mem, mem_g, wq, wk, wv, wo):
    B, S, _ = xn.shape
    memn = rmsnorm(mem, mem_g)
    q = (xn @ wq).reshape(B, S, XATTN_HEADS, XATTN_HEAD_DIM)
    k = (memn @ wk).reshape(B, N_MEM, XATTN_HEADS, XATTN_HEAD_DIM)
    v = (memn @ wv).reshape(B, N_MEM, XATTN_HEADS, XATTN_HEAD_DIM)
    s = jnp.einsum('bshd,bmhd->bhsm', q, k).astype(jnp.float32) * (XATTN_HEAD_DIM ** -0.5)
    pr = jax.nn.softmax(s, axis=-1).astype(v.dtype)
    o = jnp.einsum('bhsm,bmhd->bshd', pr, v).reshape(B, S, D_MODEL)
    return o @ wo


def conv_ffn(xn, w_up, conv_w, conv_b, w_down):
    S = xn.shape[1]
    h = xn @ w_up
    hp = jnp.pad(h, ((0, 0), (CONV_WIDTH - 1, 0), (0, 0)))
    hc = conv_b + sum(conv_w[k] * hp[:, k:k + S] for k in range(CONV_WIDTH))
    gate, val = jnp.split(hc, 2, axis=-1)
    return (jax.nn.silu(gate) * val) @ w_down


def setup_inputs(seed: int = 0) -> dict:
    key = jax.random.key(seed)
    ks = jax.random.split(key, 24)
    f32 = jnp.float32
    n = lambda k, shape, s: (jax.random.normal(k, shape, f32) * s)
    gain = lambda k, shape: 1.0 + 0.05 * jax.random.normal(k, shape, f32)
    L = DEPTH
    return {
        "x": jax.random.normal(ks[0], (BATCH, SEQ, D_MODEL), f32),
        "mem": jax.random.normal(ks[1], (BATCH, N_MEM, D_MODEL), f32),
        "norm_mix_g": gain(ks[2], (L, D_MODEL)),
        "w_in": n(ks[3], (L, D_MODEL, D_IN_PROJ), D_MODEL ** -0.5),
        "pool_w": n(ks[4], (L, POOL_GROUPS, POOL_GROUP_DIM, POOL_GROUP_DIM), POOL_GROUP_DIM ** -0.5),
        "pool_scale": 1.0 + 0.1 * jax.random.normal(ks[5], (L, POOL_WIDTH), f32),
        "sgu_g": gain(ks[6], (L, SGU_WIDTH)),
        "sgu_w": n(ks[7], (L, SGU_HEADS, CHUNK, CHUNK), CHUNK ** -0.5),
        "sgu_b": 1.0 + 0.05 * jax.random.normal(ks[8], (L, SGU_HEADS, CHUNK), f32),
        "w_out": n(ks[9], (L, D_MIX, D_MODEL), D_MIX ** -0.5),
        "norm_xattn_g": gain(ks[10], (L, D_MODEL)),
        "mem_norm_g": gain(ks[11], (L, D_MODEL)),
        "wq": n(ks[12], (L, D_MODEL, D_MODEL), D_MODEL ** -0.5),
        "wk": n(ks[13], (L, D_MODEL, D_MODEL), D_MODEL ** -0.5),
        "wv": n(ks[14], (L, D_MODEL, D_MODEL), D_MODEL ** -0.5),
        "wo": n(ks[15], (L, D_MODEL, D_MODEL), D_MODEL ** -0.5),
        "norm_ffn_g": gain(ks[16], (L, D_MODEL)),
        "w_up": n(ks[17], (L, D_MODEL, 2 * D_FF), D_MODEL ** -0.5),
        "conv_w": n(ks[18], (L, CONV_WIDTH, 2 * D_FF), CONV_WIDTH ** -0.5),
        "conv_b": n(ks[19], (L, 2 * D_FF), 0.02),
        "w_down": n(ks[20], (L, D_FF, D_MODEL), D_FF ** -0.5),
        "final_norm_g": gain(ks[21], (D_MODEL,)),
    }


def reference(x, mem, norm_mix_g, w_in, pool_w, pool_scale, sgu_g, sgu_w, sgu_b, w_out,
              norm_xattn_g, mem_norm_g, wq, wk, wv, wo,
              norm_ffn_g, w_up, conv_w, conv_b, w_down, final_norm_g):
    h = x
    for l in range(DEPTH):
        xn = rmsnorm(h, norm_mix_g[l])
        proj = xn @ w_in[l]
        p = proj[..., :POOL_WIDTH]
        uv = jax.nn.gelu(proj[..., POOL_WIDTH:], approximate=False)
        u, v = uv[..., :SGU_WIDTH], uv[..., SGU_WIDTH:]
        y_pool = pool_mixer(p, pool_w[l], pool_scale[l])
        y_sgu = sgu_mixer(u, v, sgu_g[l], sgu_w[l], sgu_b[l])
        h = h + jnp.concatenate([y_pool, y_sgu], axis=-1) @ w_out[l]
        xn = rmsnorm(h, norm_xattn_g[l])
        h = h + mem_cross_attention(xn, mem, mem_norm_g[l], wq[l], wk[l], wv[l], wo[l])
        xn = rmsnorm(h, norm_ffn_g[l])
        h = h + conv_ffn(xn, w_up[l], conv_w[l], conv_b[l], w_down[l])
    return rmsnorm(h, final_norm_g)
```

```python
from functools import partial

import jax
import jax.numpy as jnp
from jax import lax
from jax.experimental import pallas as pl
from jax.experimental.pallas import tpu as pltpu

D_MODEL = 1024
N_MEM = 256
POOL_WIDTH = 512
POOL_WINDOWS = (2, 4, 8, 16)
POOL_GROUP_DIM = 128
POOL_HALO = 16
SGU_WIDTH = 512
SGU_HEADS = 4
SGU_HEAD_DIM = 128
CHUNK = 128
D_IN_PROJ = POOL_WIDTH + 2 * SGU_WIDTH
XATTN_HEADS = 4
XATTN_HEAD_DIM = 256
D_FF = 2816
CONV_WIDTH = 3
EPS = 1e-6

ROW_TILE = 512
FF_CHUNK = 256
CARRY_ROWS = 8
VMEM_LIMIT_BYTES = 56 * 1024 * 1024

BF16 = jnp.bfloat16
F32 = jnp.float32


def _rmsnorm(x, g):
    return x * lax.rsqrt(jnp.mean(x * x, axis=-1, keepdims=True) + EPS) * g


def _gelu_exact(x):
    return 0.5 * x * (1.0 + lax.erf(x * (0.5 ** 0.5)))


def _const_spec(shape):
    return pl.BlockSpec(shape, lambda i: (0,) * len(shape), pipeline_mode=pl.Buffered(1))


def _mixer_kernel(h_ref, g_ref, w_in_ref, pool_w_ref, pool_scale_ref, sgu_g_ref, sgu_w_ref, sgu_b_ref,
                  w_out_ref, o_ref, pbuf_ref, ycat_ref, *, tiles_per_seq):
    ts = h_ref.shape[0]
    i = pl.program_id(0)
    tile_in_seq = i % tiles_per_seq

    @pl.when(tile_in_seq == 0)
    def _():
        pbuf_ref[0:POOL_HALO, :] = jnp.zeros((POOL_HALO, POOL_WIDTH), F32)

    x = h_ref[...]
    xn = _rmsnorm(x, g_ref[...]).astype(BF16)
    proj = jnp.dot(xn, w_in_ref[...], preferred_element_type=F32)

    p = proj[:, :POOL_WIDTH]
    pbuf_ref[POOL_HALO:, :] = p
    row = lax.broadcasted_iota(jnp.int32, (ts, 1), 0) + tile_in_seq * ts + 1
    for gi, win in enumerate(POOL_WINDOWS):
        lanes = slice(gi * POOL_GROUP_DIM, (gi + 1) * POOL_GROUP_DIM)
        s = pbuf_ref[:, lanes]
        shift = 1
        while shift < win:
            s = s + pltpu.roll(s, shift, 0)
            shift *= 2
        inv_count = 1.0 / jnp.minimum(row, win).astype(F32)
        d = s[POOL_HALO:, :] * inv_count - p[:, lanes]
        y = jnp.dot(d.astype(BF16), pool_w_ref[gi], preferred_element_type=F32)
        ycat_ref[:, lanes] = (y * pool_scale_ref[:, lanes]).astype(BF16)
    carry = pbuf_ref[ts:, :]
    pbuf_ref[0:POOL_HALO, :] = carry

    u = _gelu_exact(proj[:, POOL_WIDTH:POOL_WIDTH + SGU_WIDTH])
    v = _gelu_exact(proj[:, POOL_WIDTH + SGU_WIDTH:])
    mu = jnp.mean(v, axis=-1, keepdims=True)
    vc = v - mu
    vn = (vc * lax.rsqrt(jnp.mean(vc * vc, axis=-1, keepdims=True) + EPS) * sgu_g_ref[...]).astype(BF16)
    tri = (lax.broadcasted_iota(jnp.int32, (CHUNK, CHUNK), 0)
           >= lax.broadcasted_iota(jnp.int32, (CHUNK, CHUNK), 1))
    for hd in range(SGU_HEADS):
        lanes = slice(hd * SGU_HEAD_DIM, (hd + 1) * SGU_HEAD_DIM)
        w_masked = jnp.where(tri, sgu_w_ref[hd], jnp.zeros((), BF16))
        bias = sgu_b_ref[:, lanes]
        for c in range(ts // CHUNK):
            rows = slice(c * CHUNK, (c + 1) * CHUNK)
            z = jnp.dot(w_masked, vn[rows, lanes], preferred_element_type=F32) + bias
            ycat_ref[rows, POOL_WIDTH + hd * SGU_HEAD_DIM:POOL_WIDTH + (hd + 1) * SGU_HEAD_DIM] = (
                u[rows, lanes] * z).astype(BF16)

    o_ref[...] = x + jnp.dot(ycat_ref[...], w_out_ref[...], preferred_element_type=F32)


def _mixer(h, g, w_in, pool_w, pool_scale, sgu_g, sgu_w, sgu_b_full, w_out, *, seq):
    rows = h.shape[0]
    ts = ROW_TILE
    return pl.pallas_call(
        partial(_mixer_kernel, tiles_per_seq=seq // ts),
        out_shape=jax.ShapeDtypeStruct(h.shape, h.dtype),
        grid=(rows // ts,),
        in_specs=[
            pl.BlockSpec((ts, D_MODEL), lambda i: (i, 0)),
            _const_spec((1, D_MODEL)),
            _const_spec((D_MODEL, D_IN_PROJ)),
            _const_spec((len(POOL_WINDOWS), POOL_GROUP_DIM, POOL_GROUP_DIM)),
            _const_spec((1, POOL_WIDTH)),
            _const_spec((1, SGU_WIDTH)),
            _const_spec((SGU_HEADS, CHUNK, CHUNK)),
            _const_spec((CHUNK, SGU_WIDTH)),
            _const_spec((D_MODEL, D_MODEL)),
        ],
        out_specs=pl.BlockSpec((ts, D_MODEL), lambda i: (i, 0)),
        scratch_shapes=[
            pltpu.VMEM((POOL_HALO + ts, POOL_WIDTH), F32),
            pltpu.VMEM((ts, D_MODEL), BF16),
        ],
        compiler_params=pltpu.CompilerParams(
            dimension_semantics=("arbitrary",), vmem_limit_bytes=VMEM_LIMIT_BYTES),
        name="mixer",
    )(h, g, w_in, pool_w, pool_scale, sgu_g, sgu_w, sgu_b_full, w_out)


def _memkv_kernel(mem_ref, g_ref, wk_ref, wv_ref, kt_ref, v_ref):
    memn = _rmsnorm(mem_ref[0], g_ref[0]).astype(BF16)
    k = jnp.dot(memn, wk_ref[0], preferred_element_type=F32)
    kt_ref[0, 0] = k.T.astype(BF16)
    v_ref[0, 0] = jnp.dot(memn, wv_ref[0], preferred_element_type=F32).astype(BF16)


def _memkv(mem, mem_g, wk, wv):
    depth = wk.shape[0]
    batch = mem.shape[0]
    return pl.pallas_call(
        _memkv_kernel,
        out_shape=(jax.ShapeDtypeStruct((depth, batch, D_MODEL, N_MEM), BF16),
                   jax.ShapeDtypeStruct((depth, batch, N_MEM, D_MODEL), BF16)),
        grid=(depth, batch),
        in_specs=[
            pl.BlockSpec((1, N_MEM, D_MODEL), lambda l, b: (b, 0, 0)),
            pl.BlockSpec((1, 1, D_MODEL), lambda l, b: (l, 0, 0)),
            pl.BlockSpec((1, D_MODEL, D_MODEL), lambda l, b: (l, 0, 0)),
            pl.BlockSpec((1, D_MODEL, D_MODEL), lambda l, b: (l, 0, 0)),
        ],
        out_specs=(pl.BlockSpec((1, 1, D_MODEL, N_MEM), lambda l, b: (l, b, 0, 0)),
                   pl.BlockSpec((1, 1, N_MEM, D_MODEL), lambda l, b: (l, b, 0, 0))),
        compiler_params=pltpu.CompilerParams(
            dimension_semantics=("arbitrary", "arbitrary"), vmem_limit_bytes=VMEM_LIMIT_BYTES),
        name="memkv",
    )(mem, mem_g, wk, wv)


def _xattn_kernel(h_ref, g_ref, wq_ref, kt_ref, v_ref, wo_ref, o_ref, ocat_ref):
    x = h_ref[...]
    xn = _rmsnorm(x, g_ref[...]).astype(BF16)
    q = jnp.dot(xn, wq_ref[...], preferred_element_type=F32).astype(BF16)
    for hd in range(XATTN_HEADS):
        cols = slice(hd * XATTN_HEAD_DIM, (hd + 1) * XATTN_HEAD_DIM)
        s = jnp.dot(q[:, cols], kt_ref[0, 0, cols, :], preferred_element_type=F32) * (XATTN_HEAD_DIM ** -0.5)
        e = jnp.exp(s - jnp.max(s, axis=-1, keepdims=True))
        pr = (e / jnp.sum(e, axis=-1, keepdims=True)).astype(BF16)
        ocat_ref[:, cols] = jnp.dot(pr, v_ref[0, 0, :, cols], preferred_element_type=F32).astype(BF16)
    o_ref[...] = x + jnp.dot(ocat_ref[...], wo_ref[...], preferred_element_type=F32)


def _xattn(h, g, wq, kt, v, wo, *, layer, seq):
    rows = h.shape[0]
    ts = ROW_TILE
    tiles_per_seq = seq // ts
    return pl.pallas_call(
        _xattn_kernel,
        out_shape=jax.ShapeDtypeStruct(h.shape, h.dtype),
        grid=(rows // ts,),
        in_specs=[
            pl.BlockSpec((ts, D_MODEL), lambda i: (i, 0)),
            _const_spec((1, D_MODEL)),
            _const_spec((D_MODEL, D_MODEL)),
            pl.BlockSpec((1, 1, D_MODEL, N_MEM), lambda i: (layer, i // tiles_per_seq, 0, 0)),
            pl.BlockSpec((1, 1, N_MEM, D_MODEL), lambda i: (layer, i // tiles_per_seq, 0, 0)),
            _const_spec((D_MODEL, D_MODEL)),
        ],
        out_specs=pl.BlockSpec((ts, D_MODEL), lambda i: (i, 0)),
        scratch_shapes=[pltpu.VMEM((ts, D_MODEL), BF16)],
        compiler_params=pltpu.CompilerParams(
            dimension_semantics=("arbitrary",), vmem_limit_bytes=VMEM_LIMIT_BYTES),
        name="xattn",
    )(h, g, wq, kt, v, wo)


def _ffn_kernel(h_ref, g_ref, w_up_ref, conv_w_ref, conv_b_ref, w_down_ref, fg_ref, o_ref, carry_ref,
                *, tiles_per_seq, final_norm):
    ts = h_ref.shape[0]
    i = pl.program_id(0)

    @pl.when(i % tiles_per_seq == 0)
    def _():
        carry_ref[...] = jnp.zeros(carry_ref.shape, F32)

    x = h_ref[...]
    xn = _rmsnorm(x, g_ref[...]).astype(BF16)
    row8 = lax.broadcasted_iota(jnp.int32, (CARRY_ROWS, 1), 0)

    def conv(cols):
        up = jnp.dot(xn, w_up_ref[:, cols], preferred_element_type=F32)
        prev = carry_ref[:, cols]
        carry_ref[:, cols] = up[ts - CARRY_ROWS:, :]
        b, w0, w1, w2 = conv_b_ref[:, cols], conv_w_ref[0:1, cols], conv_w_ref[1:2, cols], conv_w_ref[2:3, cols]
        up1 = pltpu.roll(up, 1, 0)
        up2 = pltpu.roll(up, 2, 0)
        hc = b + w0 * up2 + w1 * up1 + w2 * up
        top1 = jnp.where(row8 < 1, pltpu.roll(prev, 1, 0), up1[:CARRY_ROWS])
        top2 = jnp.where(row8 < 2, pltpu.roll(prev, 2, 0), up2[:CARRY_ROWS])
        hc_top = b + w0 * top2 + w1 * top1 + w2 * up[:CARRY_ROWS]
        return jnp.concatenate([hc_top, hc[CARRY_ROWS:]], axis=0)

    acc = jnp.zeros((ts, D_MODEL), F32)
    for j in range(D_FF // FF_CHUNK):
        gate = conv(slice(j * FF_CHUNK, (j + 1) * FF_CHUNK))
        val = conv(slice(D_FF + j * FF_CHUNK, D_FF + (j + 1) * FF_CHUNK))
        act = (gate * (1.0 / (1.0 + jnp.exp(-gate))) * val).astype(BF16)
        acc = acc + jnp.dot(act, w_down_ref[j * FF_CHUNK:(j + 1) * FF_CHUNK, :], preferred_element_type=F32)
    y = x + acc
    if final_norm:
        y = _rmsnorm(y, fg_ref[...])
    o_ref[...] = y


def _ffn(h, g, w_up, conv_w, conv_b, w_down, final_g, *, seq, final_norm):
    rows = h.shape[0]
    ts = ROW_TILE
    return pl.pallas_call(
        partial(_ffn_kernel, tiles_per_seq=seq // ts, final_norm=final_norm),
        out_shape=jax.ShapeDtypeStruct(h.shape, h.dtype),
        grid=(rows // ts,),
        in_specs=[
            pl.BlockSpec((ts, D_MODEL), lambda i: (i, 0)),
            _const_spec((1, D_MODEL)),
            _const_spec((D_MODEL, 2 * D_FF)),
            _const_spec((CONV_WIDTH, 2 * D_FF)),
            _const_spec((1, 2 * D_FF)),
            _const_spec((D_FF, D_MODEL)),
            _const_spec((1, D_MODEL)),
        ],
        out_specs=pl.BlockSpec((ts, D_MODEL), lambda i: (i, 0)),
        scratch_shapes=[pltpu.VMEM((CARRY_ROWS, 2 * D_FF), F32)],
        compiler_params=pltpu.CompilerParams(
            dimension_semantics=("arbitrary",), vmem_limit_bytes=VMEM_LIMIT_BYTES),
        name="conv_ffn",
    )(h, g, w_up, conv_w, conv_b, w_down, final_g)


def kernel(x, mem, norm_mix_g, w_in, pool_w, pool_scale, sgu_g, sgu_w, sgu_b, w_out, norm_xattn_g, mem_norm_g,
           wq, wk, wv, wo, norm_ffn_g, w_up, conv_w, conv_b, w_down, final_norm_g):
    batch, seq, d_model = x.shape
    depth = w_in.shape[0]
    assert d_model == D_MODEL and seq % ROW_TILE == 0 and ROW_TILE % CHUNK == 0 and D_FF % FF_CHUNK == 0

    kt, v = _memkv(mem, mem_norm_g[:, None, :], wk.astype(BF16), wv.astype(BF16))
    sgu_b_full = jnp.repeat(jnp.swapaxes(sgu_b, 1, 2), SGU_HEAD_DIM, axis=2)

    h = x.reshape(batch * seq, d_model)
    for l in range(depth):
        h = _mixer(h, norm_mix_g[l][None], w_in[l].astype(BF16), pool_w[l].astype(BF16), pool_scale[l][None],
                   sgu_g[l][None], sgu_w[l].astype(BF16), sgu_b_full[l], w_out[l].astype(BF16), seq=seq)
        h = _xattn(h, norm_xattn_g[l][None], wq[l].astype(BF16), kt, v, wo[l].astype(BF16), layer=l, seq=seq)
        h = _ffn(h, norm_ffn_g[l][None], w_up[l].astype(BF16), conv_w[l], conv_b[l][None],
                 w_down[l].astype(BF16), final_norm_g[None], seq=seq, final_norm=(l == depth - 1))
    return h.reshape(batch, seq, d_model)
```

```python
from functools import partial

import jax
import jax.numpy as jnp
from jax import lax
from jax.experimental import pallas as pl
from jax.experimental.pallas import tpu as pltpu

D_MODEL = 1024
N_MEM = 256
POOL_WIDTH = 512
POOL_WINDOWS = (2, 4, 8, 16)
POOL_GROUP_DIM = 128
POOL_HALO = 16
SGU_WIDTH = 512
SGU_HEADS = 4
SGU_HEAD_DIM = 128
CHUNK = 128
D_IN_PROJ = POOL_WIDTH + 2 * SGU_WIDTH
XATTN_HEADS = 4
XATTN_HEAD_DIM = 256
D_FF = 2816
CONV_WIDTH = 3
EPS = 1e-6

ROW_TILE = 512
FF_CHUNK = 256
LANES = 128
CARRY_ROWS = 8
VMEM_LIMIT_BYTES = 56 * 1024 * 1024

BF16 = jnp.bfloat16
F32 = jnp.float32


def _rmsnorm(x, g):
    return x * lax.rsqrt(jnp.mean(x * x, axis=-1, keepdims=True) + EPS) * g


def _gelu_exact(x):
    return 0.5 * x * (1.0 + lax.erf(x * (0.5 ** 0.5)))


def _const_spec(shape):
    return pl.BlockSpec(shape, lambda i: (0,) * len(shape), pipeline_mode=pl.Buffered(1))


def _mixer_kernel(h_ref, g_ref, w_in_ref, pool_w_ref, pool_scale_ref, sgu_g_ref, sgu_w_ref, sgu_b_ref,
                  w_out_ref, o_ref, pbuf_ref, ycat_ref, *, tiles_per_seq):
    ts = h_ref.shape[0]
    i = pl.program_id(0)
    tile_in_seq = i % tiles_per_seq

    @pl.when(tile_in_seq == 0)
    def _():
        pbuf_ref[0:POOL_HALO, :] = jnp.zeros((POOL_HALO, POOL_WIDTH), F32)

    x = h_ref[...]
    xn = _rmsnorm(x, g_ref[...]).astype(BF16)
    proj = jnp.dot(xn, w_in_ref[...], preferred_element_type=F32)

    p = proj[:, :POOL_WIDTH]
    pbuf_ref[POOL_HALO:, :] = p
    row = lax.broadcasted_iota(jnp.int32, (ts, 1), 0) + tile_in_seq * ts + 1
    for gi, win in enumerate(POOL_WINDOWS):
        lanes = slice(gi * POOL_GROUP_DIM, (gi + 1) * POOL_GROUP_DIM)
        s = pbuf_ref[:, lanes]
        shift = 1
        while shift < win:
            s = s + pltpu.roll(s, shift, 0)
            shift *= 2
        inv_count = 1.0 / jnp.minimum(row, win).astype(F32)
        d = s[POOL_HALO:, :] * inv_count - p[:, lanes]
        y = jnp.dot(d.astype(BF16), pool_w_ref[gi], preferred_element_type=F32)
        ycat_ref[:, lanes] = (y * pool_scale_ref[:, lanes]).astype(BF16)
    carry = pbuf_ref[ts:, :]
    pbuf_ref[0:POOL_HALO, :] = carry

    u = _gelu_exact(proj[:, POOL_WIDTH:POOL_WIDTH + SGU_WIDTH])
    v = _gelu_exact(proj[:, POOL_WIDTH + SGU_WIDTH:])
    mu = jnp.mean(v, axis=-1, keepdims=True)
    vc = v - mu
    vn = (vc * lax.rsqrt(jnp.mean(vc * vc, axis=-1, keepdims=True) + EPS) * sgu_g_ref[...]).astype(BF16)
    tri = (lax.broadcasted_iota(jnp.int32, (CHUNK, CHUNK), 0)
           >= lax.broadcasted_iota(jnp.int32, (CHUNK, CHUNK), 1))
    for hd in range(SGU_HEADS):
        lanes = slice(hd * SGU_HEAD_DIM, (hd + 1) * SGU_HEAD_DIM)
        w_masked = jnp.where(tri, sgu_w_ref[hd], jnp.zeros((), BF16))
        bias = sgu_b_ref[:, lanes]
        for c in range(ts // CHUNK):
            rows = slice(c * CHUNK, (c + 1) * CHUNK)
            z = jnp.dot(w_masked, vn[rows, lanes], preferred_element_type=F32) + bias
            ycat_ref[rows, POOL_WIDTH + hd * SGU_HEAD_DIM:POOL_WIDTH + (hd + 1) * SGU_HEAD_DIM] = (
                u[rows, lanes] * z).astype(BF16)

    o_ref[...] = x + jnp.dot(ycat_ref[...], w_out_ref[...], preferred_element_type=F32)


def _mixer(h, g, w_in, pool_w, pool_scale, sgu_g, sgu_w, sgu_b_full, w_out, *, seq):
    rows = h.shape[0]
    ts = ROW_TILE
    return pl.pallas_call(
        partial(_mixer_kernel, tiles_per_seq=seq // ts),
        out_shape=jax.ShapeDtypeStruct(h.shape, h.dtype),
        grid=(rows // ts,),
        in_specs=[
            pl.BlockSpec((ts, D_MODEL), lambda i: (i, 0)),
            _const_spec((1, D_MODEL)),
            _const_spec((D_MODEL, D_IN_PROJ)),
            _const_spec((len(POOL_WINDOWS), POOL_GROUP_DIM, POOL_GROUP_DIM)),
            _const_spec((1, POOL_WIDTH)),
            _const_spec((1, SGU_WIDTH)),
            _const_spec((SGU_HEADS, CHUNK, CHUNK)),
            _const_spec((CHUNK, SGU_WIDTH)),
            _const_spec((D_MODEL, D_MODEL)),
        ],
        out_specs=pl.BlockSpec((ts, D_MODEL), lambda i: (i, 0)),
        scratch_shapes=[
            pltpu.VMEM((POOL_HALO + ts, POOL_WIDTH), F32),
            pltpu.VMEM((ts, D_MODEL), BF16),
        ],
        compiler_params=pltpu.CompilerParams(
            dimension_semantics=("arbitrary",), vmem_limit_bytes=VMEM_LIMIT_BYTES),
        name="mixer",
    )(h, g, w_in, pool_w, pool_scale, sgu_g, sgu_w, sgu_b_full, w_out)


def _memkv_kernel(mem_ref, g_ref, wk_ref, wv_ref, kt_ref, v_ref):
    memn = _rmsnorm(mem_ref[0], g_ref[0]).astype(BF16)
    k = jnp.dot(memn, wk_ref[0], preferred_element_type=F32)
    kt_ref[0, 0] = k.T.astype(BF16)
    v_ref[0, 0] = jnp.dot(memn, wv_ref[0], preferred_element_type=F32).astype(BF16)


def _memkv(mem, mem_g, wk, wv):
    depth = wk.shape[0]
    batch = mem.shape[0]
    return pl.pallas_call(
        _memkv_kernel,
        out_shape=(jax.ShapeDtypeStruct((depth, batch, D_MODEL, N_MEM), BF16),
                   jax.ShapeDtypeStruct((depth, batch, N_MEM, D_MODEL), BF16)),
        grid=(depth, batch),
        in_specs=[
            pl.BlockSpec((1, N_MEM, D_MODEL), lambda l, b: (b, 0, 0)),
            pl.BlockSpec((1, 1, D_MODEL), lambda l, b: (l, 0, 0)),
            pl.BlockSpec((1, D_MODEL, D_MODEL), lambda l, b: (l, 0, 0)),
            pl.BlockSpec((1, D_MODEL, D_MODEL), lambda l, b: (l, 0, 0)),
        ],
        out_specs=(pl.BlockSpec((1, 1, D_MODEL, N_MEM), lambda l, b: (l, b, 0, 0)),
                   pl.BlockSpec((1, 1, N_MEM, D_MODEL), lambda l, b: (l, b, 0, 0))),
        compiler_params=pltpu.CompilerParams(
            dimension_semantics=("arbitrary", "arbitrary"), vmem_limit_bytes=VMEM_LIMIT_BYTES),
        name="memkv",
    )(mem, mem_g, wk, wv)


def _xattn_kernel(h_ref, g_ref, wq_ref, kt_ref, v_ref, wo_ref, o_ref, ocat_ref):
    x = h_ref[...]
    xn = _rmsnorm(x, g_ref[...]).astype(BF16)
    q = jnp.dot(xn, wq_ref[...], preferred_element_type=F32).astype(BF16)
    for hd in range(XATTN_HEADS):
        cols = slice(hd * XATTN_HEAD_DIM, (hd + 1) * XATTN_HEAD_DIM)
        s = jnp.dot(q[:, cols], kt_ref[0, 0, cols, :], preferred_element_type=F32) * (XATTN_HEAD_DIM ** -0.5)
        e = jnp.exp(s - jnp.max(s, axis=-1, keepdims=True))
        pr = (e / jnp.sum(e, axis=-1, keepdims=True)).astype(BF16)
        ocat_ref[:, cols] = jnp.dot(pr, v_ref[0, 0, :, cols], preferred_element_type=F32).astype(BF16)
    o_ref[...] = x + jnp.dot(ocat_ref[...], wo_ref[...], preferred_element_type=F32)


def _xattn(h, g, wq, kt, v, wo, *, layer, seq):
    rows = h.shape[0]
    ts = ROW_TILE
    tiles_per_seq = seq // ts
    return pl.pallas_call(
        _xattn_kernel,
        out_shape=jax.ShapeDtypeStruct(h.shape, h.dtype),
        grid=(rows // ts,),
        in_specs=[
            pl.BlockSpec((ts, D_MODEL), lambda i: (i, 0)),
            _const_spec((1, D_MODEL)),
            _const_spec((D_MODEL, D_MODEL)),
            pl.BlockSpec((1, 1, D_MODEL, N_MEM), lambda i: (layer, i // tiles_per_seq, 0, 0)),
            pl.BlockSpec((1, 1, N_MEM, D_MODEL), lambda i: (layer, i // tiles_per_seq, 0, 0)),
            _const_spec((D_MODEL, D_MODEL)),
        ],
        out_specs=pl.BlockSpec((ts, D_MODEL), lambda i: (i, 0)),
        scratch_shapes=[pltpu.VMEM((ts, D_MODEL), BF16)],
        compiler_params=pltpu.CompilerParams(
            dimension_semantics=("arbitrary",), vmem_limit_bytes=VMEM_LIMIT_BYTES),
        name="xattn",
    )(h, g, wq, kt, v, wo)


def _ffn_kernel(h_ref, g_ref, w_up_ref, conv_w_ref, conv_b_ref, w_down_ref, fg_ref, o_ref,
                xn_ref, up_ref, act_ref, *, tiles_per_seq, final_norm):
    ts = h_ref.shape[0]
    i = pl.program_id(0)
    slabs_per_half = FF_CHUNK // LANES

    @pl.when(i % tiles_per_seq == 0)
    def _():
        up_ref[:, 0:CARRY_ROWS, :] = jnp.zeros((up_ref.shape[0], CARRY_ROWS, LANES), F32)

    xn_ref[...] = _rmsnorm(h_ref[...], g_ref[...]).astype(BF16)

    def conv(slab, col):
        cols = slice(col, col + LANES)
        cur = up_ref[slab, CARRY_ROWS:CARRY_ROWS + ts, :]
        prev1 = up_ref[slab, CARRY_ROWS - 1:CARRY_ROWS - 1 + ts, :]
        prev2 = up_ref[slab, CARRY_ROWS - 2:CARRY_ROWS - 2 + ts, :]
        return (conv_b_ref[:, cols] + conv_w_ref[0:1, cols] * prev2 + conv_w_ref[1:2, cols] * prev1
                + conv_w_ref[2:3, cols] * cur)

    for j in range(D_FF // FF_CHUNK):
        c0 = j * 2 * FF_CHUNK
        up = jnp.dot(xn_ref[...], w_up_ref[:, c0:c0 + 2 * FF_CHUNK], preferred_element_type=F32)
        for c in range(2 * slabs_per_half):
            up_ref[2 * slabs_per_half * j + c, CARRY_ROWS:CARRY_ROWS + ts, :] = up[:, c * LANES:(c + 1) * LANES]
        for c in range(slabs_per_half):
            s_gate = 2 * slabs_per_half * j + c
            gate = conv(s_gate, c0 + c * LANES)
            val = conv(s_gate + slabs_per_half, c0 + FF_CHUNK + c * LANES)
            act = gate * (1.0 / (1.0 + jnp.exp(-gate))) * val
            act_ref[:, j * FF_CHUNK + c * LANES:j * FF_CHUNK + (c + 1) * LANES] = act.astype(BF16)

    up_ref[:, 0:CARRY_ROWS, :] = up_ref[:, ts:ts + CARRY_ROWS, :]
    y = h_ref[...] + jnp.dot(act_ref[...], w_down_ref[...], preferred_element_type=F32)
    if final_norm:
        y = _rmsnorm(y, fg_ref[...])
    o_ref[...] = y


def _ffn(h, g, w_up, conv_w, conv_b, w_down, final_g, *, seq, final_norm):
    rows = h.shape[0]
    ts = ROW_TILE
    return pl.pallas_call(
        partial(_ffn_kernel, tiles_per_seq=seq // ts, final_norm=final_norm),
        out_shape=jax.ShapeDtypeStruct(h.shape, h.dtype),
        grid=(rows // ts,),
        in_specs=[
            pl.BlockSpec((ts, D_MODEL), lambda i: (i, 0)),
            _const_spec((1, D_MODEL)),
            _const_spec((D_MODEL, 2 * D_FF)),
            _const_spec((CONV_WIDTH, 2 * D_FF)),
            _const_spec((1, 2 * D_FF)),
            _const_spec((D_FF, D_MODEL)),
            _const_spec((1, D_MODEL)),
        ],
        out_specs=pl.BlockSpec((ts, D_MODEL), lambda i: (i, 0)),
        scratch_shapes=[
            pltpu.VMEM((ts, D_MODEL), BF16),
            pltpu.VMEM((2 * D_FF // LANES, CARRY_ROWS + ts, LANES), F32),
            pltpu.VMEM((ts, D_FF), BF16),
        ],
        compiler_params=pltpu.CompilerParams(
            dimension_semantics=("arbitrary",), vmem_limit_bytes=VMEM_LIMIT_BYTES),
        name="conv_ffn",
    )(h, g, w_up, conv_w, conv_b, w_down, final_g)


def _pair_gate_val(a):
    lead = a.shape[:-1]
    a = a.reshape(lead + (2, D_FF // FF_CHUNK, FF_CHUNK))
    return jnp.swapaxes(a, -3, -2).reshape(lead + (2 * D_FF,))


def kernel(x, mem, norm_mix_g, w_in, pool_w, pool_scale, sgu_g, sgu_w, sgu_b, w_out, norm_xattn_g, mem_norm_g,
           wq, wk, wv, wo, norm_ffn_g, w_up, conv_w, conv_b, w_down, final_norm_g):
    batch, seq, d_model = x.shape
    depth = w_in.shape[0]
    assert d_model == D_MODEL and seq % ROW_TILE == 0 and ROW_TILE % CHUNK == 0 and D_FF % FF_CHUNK == 0

    kt, v = _memkv(mem, mem_norm_g[:, None, :], wk.astype(BF16), wv.astype(BF16))
    sgu_b_full = jnp.repeat(jnp.swapaxes(sgu_b, 1, 2), SGU_HEAD_DIM, axis=2)

    h = x.reshape(batch * seq, d_model)
    for l in range(depth):
        h = _mixer(h, norm_mix_g[l][None], w_in[l].astype(BF16), pool_w[l].astype(BF16), pool_scale[l][None],
                   sgu_g[l][None], sgu_w[l].astype(BF16), sgu_b_full[l], w_out[l].astype(BF16), seq=seq)
        h = _xattn(h, norm_xattn_g[l][None], wq[l].astype(BF16), kt, v, wo[l].astype(BF16), layer=l, seq=seq)
        h = _ffn(h, norm_ffn_g[l][None], _pair_gate_val(w_up[l].astype(BF16)), _pair_gate_val(conv_w[l]),
                 _pair_gate_val(conv_b[l][None]), w_down[l].astype(BF16), final_norm_g[None], seq=seq, final_norm=(l == depth - 1))
    return h.reshape(batch, seq, d_model)
```

```python
from functools import partial

import jax
import jax.numpy as jnp
from jax import lax
from jax.experimental import pallas as pl
from jax.experimental.pallas import tpu as pltpu

D_MODEL = 1024
N_MEM = 256
POOL_WIDTH = 512
POOL_WINDOWS = (2, 4, 8, 16)
POOL_GROUP_DIM = 128
POOL_HALO = 16
SGU_WIDTH = 512
SGU_HEADS = 4
SGU_HEAD_DIM = 128
CHUNK = 128
D_IN_PROJ = POOL_WIDTH + 2 * SGU_WIDTH
XATTN_HEADS = 4
XATTN_HEAD_DIM = 256
D_FF = 2816
CONV_WIDTH = 3
EPS = 1e-6

ROW_TILE = 512
MIXER_SUBTILES = 2
XATTN_SUBTILES = 4
FF_CHUNK = 256
LANES = 128
CARRY_ROWS = 8
VMEM_LIMIT_BYTES = 56 * 1024 * 1024

BF16 = jnp.bfloat16
F32 = jnp.float32


def _rmsnorm(x, g):
    return x * lax.rsqrt(jnp.mean(x * x, axis=-1, keepdims=True) + EPS) * g


def _gelu_exact(x):
    return 0.5 * x * (1.0 + lax.erf(x * (0.5 ** 0.5)))


def _layer_spec(tail, layer):
    return pl.BlockSpec((1,) + tail, lambda i: (layer,) + (0,) * len(tail), pipeline_mode=pl.Buffered(1))


def _mixer_kernel(h_ref, g_ref, w_in_ref, sgu_g_ref, sgu_w_ref, sgu_b_ref, w_out_ref, o_ref,
                  xn_ref, u_ref, vn_ref, pbuf_ref, ycat_ref, *, tiles_per_seq):
    ts = h_ref.shape[0]
    sub = ts // MIXER_SUBTILES
    i = pl.program_id(0)
    tile_in_seq = i % tiles_per_seq
    u_cols = slice(POOL_WIDTH, POOL_WIDTH + SGU_WIDTH)
    v_cols = slice(POOL_WIDTH + SGU_WIDTH, D_IN_PROJ)
    tri = (lax.broadcasted_iota(jnp.int32, (CHUNK, CHUNK), 0)
           >= lax.broadcasted_iota(jnp.int32, (CHUNK, CHUNK), 1))

    @pl.when(tile_in_seq == 0)
    def _():
        pbuf_ref[0:POOL_HALO, :] = jnp.zeros((POOL_HALO, POOL_WIDTH), F32)

    for r in range(MIXER_SUBTILES):
        rows = slice(r * sub, (r + 1) * sub)
        prow = slice(POOL_HALO + r * sub, POOL_HALO + (r + 1) * sub)
        xn_ref[rows, :] = _rmsnorm(h_ref[rows, :], g_ref[0]).astype(BF16)

        v = _gelu_exact(jnp.dot(xn_ref[rows, :], w_in_ref[0, :, v_cols], preferred_element_type=F32))
        mu = jnp.mean(v, axis=-1, keepdims=True)
        vc = v - mu
        vn_ref[rows, :] = (vc * lax.rsqrt(jnp.mean(vc * vc, axis=-1, keepdims=True) + EPS)
                           * sgu_g_ref[0]).astype(BF16)
        pbuf_ref[prow, :] = jnp.dot(xn_ref[rows, :], w_in_ref[0, :, :POOL_WIDTH], preferred_element_type=F32)
        u_ref[rows, :] = _gelu_exact(jnp.dot(xn_ref[rows, :], w_in_ref[0, :, u_cols], preferred_element_type=F32))

        t1 = lax.broadcasted_iota(jnp.int32, (sub, 1), 0) + (tile_in_seq * ts + r * sub + 1)
        for gi, win in enumerate(POOL_WINDOWS):
            lanes = slice(gi * POOL_GROUP_DIM, (gi + 1) * POOL_GROUP_DIM)
            s = pbuf_ref[r * sub:POOL_HALO + (r + 1) * sub, lanes]
            p = s[POOL_HALO:, :]
            shift = 1
            while shift < win:
                s = s + pltpu.roll(s, shift, 0)
                shift *= 2
            inv_count = 1.0 / jnp.minimum(t1, win).astype(F32)
            ycat_ref[rows, lanes] = (s[POOL_HALO:, :] * inv_count - p).astype(BF16)

        for hd in range(SGU_HEADS):
            lanes = slice(hd * SGU_HEAD_DIM, (hd + 1) * SGU_HEAD_DIM)
            w_masked = jnp.where(tri, sgu_w_ref[0, hd], jnp.zeros((), BF16))
            bias = sgu_b_ref[0, :, lanes]
            for c in range(sub // CHUNK):
                crows = slice(r * sub + c * CHUNK, r * sub + (c + 1) * CHUNK)
                z = jnp.dot(w_masked, vn_ref[crows, lanes], preferred_element_type=F32) + bias
                ycat_ref[crows, POOL_WIDTH + hd * SGU_HEAD_DIM:POOL_WIDTH + (hd + 1) * SGU_HEAD_DIM] = (
                    u_ref[crows, lanes] * z).astype(BF16)

        o_ref[rows, :] = (h_ref[rows, :]
                          + jnp.dot(ycat_ref[rows, :POOL_WIDTH], w_out_ref[0, :POOL_WIDTH, :],
                                    preferred_element_type=F32)
                          + jnp.dot(ycat_ref[rows, POOL_WIDTH:], w_out_ref[0, POOL_WIDTH:, :],
                                    preferred_element_type=F32))

    pbuf_ref[0:POOL_HALO, :] = pbuf_ref[ts:, :]


def _mixer(h, g, w_in, sgu_g, sgu_w, sgu_b_full, w_out, *, layer, seq):
    rows = h.shape[0]
    ts = ROW_TILE * MIXER_SUBTILES
    return pl.pallas_call(
        partial(_mixer_kernel, tiles_per_seq=seq // ts),
        out_shape=jax.ShapeDtypeStruct(h.shape, h.dtype),
        grid=(rows // ts,),
        in_specs=[
            pl.BlockSpec((ts, D_MODEL), lambda i: (i, 0)),
            _layer_spec((1, D_MODEL), layer),
            _layer_spec((D_MODEL, D_IN_PROJ), layer),
            _layer_spec((1, SGU_WIDTH), layer),
            _layer_spec((SGU_HEADS, CHUNK, CHUNK), layer),
            _layer_spec((CHUNK, SGU_WIDTH), layer),
            _layer_spec((D_MODEL, D_MODEL), layer),
        ],
        out_specs=pl.BlockSpec((ts, D_MODEL), lambda i: (i, 0)),
        scratch_shapes=[
            pltpu.VMEM((ts, D_MODEL), BF16),
            pltpu.VMEM((ts, SGU_WIDTH), F32),
            pltpu.VMEM((ts, SGU_WIDTH), BF16),
            pltpu.VMEM((POOL_HALO + ts, POOL_WIDTH), F32),
            pltpu.VMEM((ts, D_MODEL), BF16),
        ],
        compiler_params=pltpu.CompilerParams(
            dimension_semantics=("arbitrary",), vmem_limit_bytes=VMEM_LIMIT_BYTES),
        name="mixer",
    )(h, g, w_in, sgu_g, sgu_w, sgu_b_full, w_out)


def _memkv_kernel(mem_ref, g_ref, wk_ref, wv_ref, wq_ref, wo_ref, wqk_ref, vo_ref):
    memn = _rmsnorm(mem_ref[0], g_ref[0]).astype(BF16)
    k = jnp.dot(memn, wk_ref[0], preferred_element_type=F32).astype(BF16)
    v = jnp.dot(memn, wv_ref[0], preferred_element_type=F32).astype(BF16)
    for hd in range(XATTN_HEADS):
        cols = slice(hd * XATTN_HEAD_DIM, (hd + 1) * XATTN_HEAD_DIM)
        qk = lax.dot_general(wq_ref[0, :, cols], k[:, cols], (((1,), (1,)), ((), ())),
                             preferred_element_type=F32)
        wqk_ref[0, 0, :, hd * N_MEM:(hd + 1) * N_MEM] = (qk * (XATTN_HEAD_DIM ** -0.5)).astype(BF16)
        vo_ref[0, 0, hd * N_MEM:(hd + 1) * N_MEM, :] = jnp.dot(
            v[:, cols], wo_ref[0, cols, :], preferred_element_type=F32).astype(BF16)


def _memkv(mem, mem_g, wk, wv, wq, wo):
    depth = wk.shape[0]
    batch = mem.shape[0]
    w_spec = pl.BlockSpec((1, D_MODEL, D_MODEL), lambda l, b: (l, 0, 0))
    return pl.pallas_call(
        _memkv_kernel,
        out_shape=(jax.ShapeDtypeStruct((depth, batch, D_MODEL, XATTN_HEADS * N_MEM), BF16),
                   jax.ShapeDtypeStruct((depth, batch, XATTN_HEADS * N_MEM, D_MODEL), BF16)),
        grid=(depth, batch),
        in_specs=[
            pl.BlockSpec((1, N_MEM, D_MODEL), lambda l, b: (b, 0, 0)),
            pl.BlockSpec((1, 1, D_MODEL), lambda l, b: (l, 0, 0)),
            w_spec, w_spec, w_spec, w_spec,
        ],
        out_specs=(pl.BlockSpec((1, 1, D_MODEL, XATTN_HEADS * N_MEM), lambda l, b: (l, b, 0, 0)),
                   pl.BlockSpec((1, 1, XATTN_HEADS * N_MEM, D_MODEL), lambda l, b: (l, b, 0, 0))),
        compiler_params=pltpu.CompilerParams(
            dimension_semantics=("arbitrary", "arbitrary"), vmem_limit_bytes=VMEM_LIMIT_BYTES),
        name="memkv",
    )(mem, mem_g, wk, wv, wq, wo)


def _xattn_kernel(h_ref, g_ref, wqk_ref, vo_ref, o_ref, pr_ref):
    sub = h_ref.shape[0] // XATTN_SUBTILES
    for r in range(XATTN_SUBTILES):
        rows = slice(r * sub, (r + 1) * sub)
        x = h_ref[rows, :]
        xn = _rmsnorm(x, g_ref[0]).astype(BF16)
        s = jnp.dot(xn, wqk_ref[0, 0], preferred_element_type=F32)
        for hd in range(XATTN_HEADS):
            cols = slice(hd * N_MEM, (hd + 1) * N_MEM)
            sh = s[:, cols]
            e = jnp.exp(sh - jnp.max(sh, axis=-1, keepdims=True))
            pr_ref[rows, cols] = (e / jnp.sum(e, axis=-1, keepdims=True)).astype(BF16)
        o_ref[rows, :] = x + jnp.dot(pr_ref[rows, :], vo_ref[0, 0], preferred_element_type=F32)


def _xattn(h, g, wqk, vo, *, layer, seq):
    rows = h.shape[0]
    ts = ROW_TILE * XATTN_SUBTILES
    tiles_per_seq = seq // ts
    kv_spec = pl.BlockSpec((1, 1) + wqk.shape[2:], lambda i: (layer, i // tiles_per_seq, 0, 0))
    return pl.pallas_call(
        _xattn_kernel,
        out_shape=jax.ShapeDtypeStruct(h.shape, h.dtype),
        grid=(rows // ts,),
        in_specs=[
            pl.BlockSpec((ts, D_MODEL), lambda i: (i, 0)),
            _layer_spec((1, D_MODEL), layer),
            kv_spec,
            kv_spec,
        ],
        out_specs=pl.BlockSpec((ts, D_MODEL), lambda i: (i, 0)),
        scratch_shapes=[pltpu.VMEM((ts, XATTN_HEADS * N_MEM), BF16)],
        compiler_params=pltpu.CompilerParams(
            dimension_semantics=("arbitrary",), vmem_limit_bytes=VMEM_LIMIT_BYTES),
        name="xattn",
    )(h, g, wqk, vo)


def _ffn_kernel(h_ref, g_ref, w_up_ref, conv_w_ref, conv_b_ref, w_down_ref, fg_ref, o_ref,
                xn_ref, up_ref, act_ref, *, tiles_per_seq, final_norm):
    ts = h_ref.shape[0]
    i = pl.program_id(0)
    slabs_per_chunk = FF_CHUNK // LANES
    val_slab0 = D_FF // LANES

    @pl.when(i % tiles_per_seq == 0)
    def _():
        up_ref[:, 0:CARRY_ROWS, :] = jnp.zeros((up_ref.shape[0], CARRY_ROWS, LANES), F32)

    xn_ref[...] = _rmsnorm(h_ref[...], g_ref[0]).astype(BF16)

    def up_project(first_slab):
        c0 = first_slab * LANES
        up = jnp.dot(xn_ref[...], w_up_ref[0, :, c0:c0 + FF_CHUNK], preferred_element_type=F32)
        for c in range(slabs_per_chunk):
            up_ref[first_slab + c, CARRY_ROWS:CARRY_ROWS + ts, :] = up[:, c * LANES:(c + 1) * LANES]

    def conv(slab):
        cols = slice(slab * LANES, (slab + 1) * LANES)
        cur = up_ref[slab, CARRY_ROWS:CARRY_ROWS + ts, :]
        prev1 = up_ref[slab, CARRY_ROWS - 1:CARRY_ROWS - 1 + ts, :]
        prev2 = up_ref[slab, CARRY_ROWS - 2:CARRY_ROWS - 2 + ts, :]
        return (conv_b_ref[0, :, cols] + conv_w_ref[0, 0:1, cols] * prev2 + conv_w_ref[0, 1:2, cols] * prev1
                + conv_w_ref[0, 2:3, cols] * cur)

    for j in range(D_FF // FF_CHUNK):
        up_project(j * slabs_per_chunk)
        up_project(val_slab0 + j * slabs_per_chunk)
        for c in range(slabs_per_chunk):
            slab = j * slabs_per_chunk + c
            gate = conv(slab)
            val = conv(val_slab0 + slab)
            act = gate * (1.0 / (1.0 + jnp.exp(-gate))) * val
            act_ref[:, slab * LANES:(slab + 1) * LANES] = act.astype(BF16)

    up_ref[:, 0:CARRY_ROWS, :] = up_ref[:, ts:ts + CARRY_ROWS, :]
    y = h_ref[...] + jnp.dot(act_ref[...], w_down_ref[0], preferred_element_type=F32)
    if final_norm:
        y = _rmsnorm(y, fg_ref[...])
    o_ref[...] = y


def _ffn(h, g, w_up, conv_w, conv_b, w_down, final_g, *, layer, seq, final_norm):
    rows = h.shape[0]
    ts = ROW_TILE
    return pl.pallas_call(
        partial(_ffn_kernel, tiles_per_seq=seq // ts, final_norm=final_norm),
        out_shape=jax.ShapeDtypeStruct(h.shape, h.dtype),
        grid=(rows // ts,),
        in_specs=[
            pl.BlockSpec((ts, D_MODEL), lambda i: (i, 0)),
            _layer_spec((1, D_MODEL), layer),
            _layer_spec((D_MODEL, 2 * D_FF), layer),
            _layer_spec((CONV_WIDTH, 2 * D_FF), layer),
            _layer_spec((1, 2 * D_FF), layer),
            _layer_spec((D_FF, D_MODEL), layer),
            pl.BlockSpec((1, D_MODEL), lambda i: (0, 0), pipeline_mode=pl.Buffered(1)),
        ],
        out_specs=pl.BlockSpec((ts, D_MODEL), lambda i: (i, 0)),
        scratch_shapes=[
            pltpu.VMEM((ts, D_MODEL), BF16),
            pltpu.VMEM((2 * D_FF // LANES, CARRY_ROWS + ts, LANES), F32),
            pltpu.VMEM((ts, D_FF), BF16),
        ],
        compiler_params=pltpu.CompilerParams(
            dimension_semantics=("arbitrary",), vmem_limit_bytes=VMEM_LIMIT_BYTES),
        name="conv_ffn",
    )(h, g, w_up, conv_w, conv_b, w_down, final_g)


def _outfold_kernel(pool_w_ref, pool_scale_ref, w_out_ref, o_ref):
    for gi in range(len(POOL_WINDOWS)):
        rows = slice(gi * POOL_GROUP_DIM, (gi + 1) * POOL_GROUP_DIM)
        a = pool_w_ref[0, gi] * pool_scale_ref[0, :, rows]
        o_ref[0, rows, :] = jnp.dot(a, w_out_ref[0, rows, :], preferred_element_type=F32,
                                    precision=lax.Precision.HIGHEST).astype(BF16)
    o_ref[0, POOL_WIDTH:, :] = w_out_ref[0, POOL_WIDTH:, :].astype(BF16)


def _outfold(pool_w, pool_scale, w_out):
    depth = w_out.shape[0]
    return pl.pallas_call(
        _outfold_kernel,
        out_shape=jax.ShapeDtypeStruct(w_out.shape, BF16),
        grid=(depth,),
        in_specs=[
            pl.BlockSpec((1,) + pool_w.shape[1:], lambda l: (l, 0, 0, 0)),
            pl.BlockSpec((1, 1, POOL_WIDTH), lambda l: (l, 0, 0)),
            pl.BlockSpec((1, D_MODEL, D_MODEL), lambda l: (l, 0, 0)),
        ],
        out_specs=pl.BlockSpec((1, D_MODEL, D_MODEL), lambda l: (l, 0, 0)),
        compiler_params=pltpu.CompilerParams(
            dimension_semantics=("arbitrary",), vmem_limit_bytes=VMEM_LIMIT_BYTES),
        name="outfold",
    )(pool_w, pool_scale, w_out)


def kernel(x, mem, norm_mix_g, w_in, pool_w, pool_scale, sgu_g, sgu_w, sgu_b, w_out, norm_xattn_g, mem_norm_g,
           wq, wk, wv, wo, norm_ffn_g, w_up, conv_w, conv_b, w_down, final_norm_g):
    batch, seq, d_model = x.shape
    depth = w_in.shape[0]
    assert d_model == D_MODEL and seq % ROW_TILE == 0 and ROW_TILE % CHUNK == 0 and D_FF % FF_CHUNK == 0

    def bf(w):
        return w.astype(BF16)

    def row(p):
        return p[:, None, :]

    wqk, vo = _memkv(mem, row(mem_norm_g), bf(wk), bf(wv), bf(wq), bf(wo))
    w_out_folded = _outfold(pool_w, row(pool_scale), w_out)
    sgu_b_full = jnp.repeat(jnp.swapaxes(sgu_b, 1, 2), SGU_HEAD_DIM, axis=2)
    w_in_b, sgu_w_b, w_up_b, w_down_b = bf(w_in), bf(sgu_w), bf(w_up), bf(w_down)

    h = x.reshape(batch * seq, d_model)
    for l in range(depth):
        h = _mixer(h, row(norm_mix_g), w_in_b, row(sgu_g), sgu_w_b, sgu_b_full, w_out_folded, layer=l, seq=seq)
        h = _xattn(h, row(norm_xattn_g), wqk, vo, layer=l, seq=seq)
        h = _ffn(h, row(norm_ffn_g), w_up_b, conv_w, row(conv_b), w_down_b, final_norm_g[None],
                 layer=l, seq=seq, final_norm=(l == depth - 1))
    return h.reshape(batch, seq, d_model)
```

```python
from functools import partial

import jax
import jax.numpy as jnp
from jax import lax
from jax.experimental import pallas as pl
from jax.experimental.pallas import tpu as pltpu

D_MODEL = 1024
N_MEM = 256
POOL_WIDTH = 512
POOL_WINDOWS = (2, 4, 8, 16)
POOL_GROUP_DIM = 128
POOL_LEVEL_BUFFERS = 6
POOL_HALO = 32
SGU_WIDTH = 512
SGU_HEADS = 4
SGU_HEAD_DIM = 128
CHUNK = 128
D_IN_PROJ = POOL_WIDTH + 2 * SGU_WIDTH
XATTN_HEADS = 4
XATTN_HEAD_DIM = 256
D_FF = 2816
CONV_WIDTH = 3
EPS = 1e-6

ROW_TILE = 512
MIXER_SUBTILES = 2
XATTN_SUBTILES = 4
FFN_ROW_TILE = 512
FF_CHUNK = 256
LANES = 128
CARRY_ROWS = 8
VMEM_LIMIT_BYTES = 56 * 1024 * 1024

BF16 = jnp.bfloat16
F32 = jnp.float32


def _rmsnorm(x, g):
    return x * lax.rsqrt(jnp.mean(x * x, axis=-1, keepdims=True) + EPS) * g


def _gelu_exact(x):
    return 0.5 * x * (1.0 + lax.erf(x * (0.5 ** 0.5)))


def _layer_spec(tail, layer):
    return pl.BlockSpec((1,) + tail, lambda i: (layer,) + (0,) * len(tail), pipeline_mode=pl.Buffered(1))


def _mixer_kernel(h_ref, g_ref, w_in_ref, sgu_g_ref, sgu_w_ref, sgu_b_ref, w_out_ref, o_ref,
                  xn_ref, u_ref, vn_ref, pbuf_ref, ptmp_ref, ycat_ref, *, tiles_per_seq):
    ts = h_ref.shape[0]
    sub = ts // MIXER_SUBTILES
    i = pl.program_id(0)
    tile_in_seq = i % tiles_per_seq
    u_cols = slice(POOL_WIDTH, POOL_WIDTH + SGU_WIDTH)
    v_cols = slice(POOL_WIDTH + SGU_WIDTH, D_IN_PROJ)
    tri = (lax.broadcasted_iota(jnp.int32, (CHUNK, CHUNK), 0)
           >= lax.broadcasted_iota(jnp.int32, (CHUNK, CHUNK), 1))

    @pl.when(tile_in_seq == 0)
    def _():
        pbuf_ref[:, 0:POOL_HALO, :] = jnp.zeros((len(POOL_WINDOWS), POOL_HALO, POOL_GROUP_DIM), F32)

    for r in range(MIXER_SUBTILES):
        rows = slice(r * sub, (r + 1) * sub)
        xn_ref[rows, :] = _rmsnorm(h_ref[rows, :], g_ref[0]).astype(BF16)

        v = _gelu_exact(jnp.dot(xn_ref[rows, :], w_in_ref[0, :, v_cols], preferred_element_type=F32))
        mu = jnp.mean(v, axis=-1, keepdims=True)
        vc = v - mu
        vn_ref[rows, :] = (vc * lax.rsqrt(jnp.mean(vc * vc, axis=-1, keepdims=True) + EPS)
                           * sgu_g_ref[0]).astype(BF16)
        p_all = jnp.dot(xn_ref[rows, :], w_in_ref[0, :, :POOL_WIDTH], preferred_element_type=F32)
        for gi in range(len(POOL_WINDOWS)):
            pbuf_ref[gi, POOL_HALO + r * sub:POOL_HALO + (r + 1) * sub, :] = (
                p_all[:, gi * POOL_GROUP_DIM:(gi + 1) * POOL_GROUP_DIM])
        u_ref[rows, :] = _gelu_exact(jnp.dot(xn_ref[rows, :], w_in_ref[0, :, u_cols], preferred_element_type=F32))

    for r in range(MIXER_SUBTILES):
        rows = slice(r * sub, (r + 1) * sub)

        base = r * sub
        n_tmp = 0
        for gi, win in enumerate(POOL_WINDOWS):
            lanes = slice(gi * POOL_GROUP_DIM, (gi + 1) * POOL_GROUP_DIM)

            def read(lo, n, gi=gi):
                return pbuf_ref[gi, base + lo:base + lo + n, :]

            span, first = 1, 0
            while 2 * span < win:
                first += 8
                n = POOL_HALO + sub - first
                tmp = ptmp_ref.at[r, n_tmp]
                n_tmp += 1
                tmp[first:first + n, :] = read(first, n) + read(first - span, n)

                def read(lo, n, tmp=tmp):
                    return tmp[lo:lo + n, :]

                span *= 2
            s = read(POOL_HALO, sub) + read(POOL_HALO - span, sub)
            p = pbuf_ref[gi, base + POOL_HALO:base + POOL_HALO + sub, :]
            if r == 0:
                nh = POOL_WINDOWS[-1]
                t1 = lax.broadcasted_iota(jnp.int32, (nh, 1), 0) + 1
                count = jnp.where(tile_in_seq == 0, jnp.minimum(t1, win), win)
                head = s[:nh, :] * (1.0 / count.astype(F32)) - p[:nh, :]
                ycat_ref[0:nh, lanes] = head.astype(BF16)
                ycat_ref[nh:sub, lanes] = (s[nh:, :] * (1.0 / win) - p[nh:, :]).astype(BF16)
            else:
                ycat_ref[rows, lanes] = (s * (1.0 / win) - p).astype(BF16)

        for hd in range(SGU_HEADS):
            lanes = slice(hd * SGU_HEAD_DIM, (hd + 1) * SGU_HEAD_DIM)
            w_masked = jnp.where(tri, sgu_w_ref[0, hd], jnp.zeros((), BF16))
            bias = sgu_b_ref[0, :, lanes]
            for c in range(sub // CHUNK):
                crows = slice(r * sub + c * CHUNK, r * sub + (c + 1) * CHUNK)
                z = jnp.dot(w_masked, vn_ref[crows, lanes], preferred_element_type=F32) + bias
                ycat_ref[crows, POOL_WIDTH + hd * SGU_HEAD_DIM:POOL_WIDTH + (hd + 1) * SGU_HEAD_DIM] = (
                    u_ref[crows, lanes] * z).astype(BF16)

        o_ref[rows, :] = (h_ref[rows, :]
                          + jnp.dot(ycat_ref[rows, :POOL_WIDTH], w_out_ref[0, :POOL_WIDTH, :],
                                    preferred_element_type=F32)
                          + jnp.dot(ycat_ref[rows, POOL_WIDTH:], w_out_ref[0, POOL_WIDTH:, :],
                                    preferred_element_type=F32))

    pbuf_ref[:, 0:POOL_HALO, :] = pbuf_ref[:, ts:, :]


def _mixer(h, g, w_in, sgu_g, sgu_w, sgu_b_full, w_out, *, layer, seq):
    rows = h.shape[0]
    ts = ROW_TILE * MIXER_SUBTILES
    return pl.pallas_call(
        partial(_mixer_kernel, tiles_per_seq=seq // ts),
        out_shape=jax.ShapeDtypeStruct(h.shape, h.dtype),
        grid=(rows // ts,),
        in_specs=[
            pl.BlockSpec((ts, D_MODEL), lambda i: (i, 0)),
            _layer_spec((1, D_MODEL), layer),
            _layer_spec((D_MODEL, D_IN_PROJ), layer),
            _layer_spec((1, SGU_WIDTH), layer),
            _layer_spec((SGU_HEADS, CHUNK, CHUNK), layer),
            _layer_spec((CHUNK, SGU_WIDTH), layer),
            _layer_spec((D_MODEL, D_MODEL), layer),
        ],
        out_specs=pl.BlockSpec((ts, D_MODEL), lambda i: (i, 0)),
        scratch_shapes=[
            pltpu.VMEM((ts, D_MODEL), BF16),
            pltpu.VMEM((ts, SGU_WIDTH), F32),
            pltpu.VMEM((ts, SGU_WIDTH), BF16),
            pltpu.VMEM((len(POOL_WINDOWS), POOL_HALO + ts, POOL_GROUP_DIM), F32),
            pltpu.VMEM((MIXER_SUBTILES, POOL_LEVEL_BUFFERS, POOL_HALO + ts // MIXER_SUBTILES, POOL_GROUP_DIM), F32),
            pltpu.VMEM((ts, D_MODEL), BF16),
        ],
        compiler_params=pltpu.CompilerParams(
            dimension_semantics=("arbitrary",), vmem_limit_bytes=VMEM_LIMIT_BYTES),
        name="mixer",
    )(h, g, w_in, sgu_g, sgu_w, sgu_b_full, w_out)


def _memkv_kernel(mem_ref, g_ref, wk_ref, wv_ref, wq_ref, wo_ref, wqk_ref, vo_ref):
    memn = _rmsnorm(mem_ref[0], g_ref[0]).astype(BF16)
    k = jnp.dot(memn, wk_ref[0], preferred_element_type=F32).astype(BF16)
    v = jnp.dot(memn, wv_ref[0], preferred_element_type=F32).astype(BF16)
    for hd in range(XATTN_HEADS):
        cols = slice(hd * XATTN_HEAD_DIM, (hd + 1) * XATTN_HEAD_DIM)
        qk = lax.dot_general(wq_ref[0, :, cols], k[:, cols], (((1,), (1,)), ((), ())),
                             preferred_element_type=F32)
        wqk_ref[0, 0, :, hd * N_MEM:(hd + 1) * N_MEM] = (qk * (XATTN_HEAD_DIM ** -0.5)).astype(BF16)
        vo_ref[0, 0, hd * N_MEM:(hd + 1) * N_MEM, :] = jnp.dot(
            v[:, cols], wo_ref[0, cols, :], preferred_element_type=F32).astype(BF16)


def _memkv(mem, mem_g, wk, wv, wq, wo):
    depth = wk.shape[0]
    batch = mem.shape[0]
    w_spec = pl.BlockSpec((1, D_MODEL, D_MODEL), lambda l, b: (l, 0, 0))
    return pl.pallas_call(
        _memkv_kernel,
        out_shape=(jax.ShapeDtypeStruct((depth, batch, D_MODEL, XATTN_HEADS * N_MEM), BF16),
                   jax.ShapeDtypeStruct((depth, batch, XATTN_HEADS * N_MEM, D_MODEL), BF16)),
        grid=(depth, batch),
        in_specs=[
            pl.BlockSpec((1, N_MEM, D_MODEL), lambda l, b: (b, 0, 0)),
            pl.BlockSpec((1, 1, D_MODEL), lambda l, b: (l, 0, 0)),
            w_spec, w_spec, w_spec, w_spec,
        ],
        out_specs=(pl.BlockSpec((1, 1, D_MODEL, XATTN_HEADS * N_MEM), lambda l, b: (l, b, 0, 0)),
                   pl.BlockSpec((1, 1, XATTN_HEADS * N_MEM, D_MODEL), lambda l, b: (l, b, 0, 0))),
        compiler_params=pltpu.CompilerParams(
            dimension_semantics=("arbitrary", "arbitrary"), vmem_limit_bytes=VMEM_LIMIT_BYTES),
        name="memkv",
    )(mem, mem_g, wk, wv, wq, wo)


def _xattn_kernel(h_ref, g_ref, wqk_ref, vo_ref, o_ref, pr_ref):
    sub = h_ref.shape[0] // XATTN_SUBTILES

    def scores(r):
        rows = slice(r * sub, (r + 1) * sub)
        xn = _rmsnorm(h_ref[rows, :], g_ref[0]).astype(BF16)
        s = jnp.dot(xn, wqk_ref[0, 0], preferred_element_type=F32)
        for hd in range(XATTN_HEADS):
            cols = slice(hd * N_MEM, (hd + 1) * N_MEM)
            sh = s[:, cols]
            e = jnp.exp(sh - jnp.max(sh, axis=-1, keepdims=True))
            pr_ref[rows, cols] = (e / jnp.sum(e, axis=-1, keepdims=True)).astype(BF16)

    def attend(r):
        rows = slice(r * sub, (r + 1) * sub)
        o_ref[rows, :] = h_ref[rows, :] + jnp.dot(pr_ref[rows, :], vo_ref[0, 0], preferred_element_type=F32)

    scores(0)
    for r in range(XATTN_SUBTILES):
        if r + 1 < XATTN_SUBTILES:
            scores(r + 1)
        attend(r)


def _xattn(h, g, wqk, vo, *, layer, seq):
    rows = h.shape[0]
    ts = ROW_TILE * XATTN_SUBTILES
    tiles_per_seq = seq // ts
    kv_spec = pl.BlockSpec((1, 1) + wqk.shape[2:], lambda i: (layer, i // tiles_per_seq, 0, 0))
    return pl.pallas_call(
        _xattn_kernel,
        out_shape=jax.ShapeDtypeStruct(h.shape, h.dtype),
        grid=(rows // ts,),
        in_specs=[
            pl.BlockSpec((ts, D_MODEL), lambda i: (i, 0)),
            _layer_spec((1, D_MODEL), layer),
            kv_spec,
            kv_spec,
        ],
        out_specs=pl.BlockSpec((ts, D_MODEL), lambda i: (i, 0)),
        scratch_shapes=[pltpu.VMEM((ts, XATTN_HEADS * N_MEM), BF16)],
        compiler_params=pltpu.CompilerParams(
            dimension_semantics=("arbitrary",), vmem_limit_bytes=VMEM_LIMIT_BYTES),
        name="xattn",
    )(h, g, wqk, vo)


def _ffn_kernel(h_ref, g_ref, w_up_ref, conv_w_ref, conv_b_ref, w_down_ref, fg_ref, o_ref,
                xn_ref, up_ref, act_ref, *, tiles_per_seq, final_norm):
    ts = h_ref.shape[0]
    i = pl.program_id(0)
    slabs_per_chunk = FF_CHUNK // LANES
    val_slab0 = D_FF // LANES

    @pl.when(i % tiles_per_seq == 0)
    def _():
        up_ref[:, 0:CARRY_ROWS, :] = jnp.zeros((up_ref.shape[0], CARRY_ROWS, LANES), F32)

    xn_ref[...] = _rmsnorm(h_ref[...], g_ref[0]).astype(BF16)

    def up_project(first_slab):
        c0 = first_slab * LANES
        up = jnp.dot(xn_ref[...], w_up_ref[0, :, c0:c0 + FF_CHUNK], preferred_element_type=F32)
        for c in range(slabs_per_chunk):
            up_ref[first_slab + c, CARRY_ROWS:CARRY_ROWS + ts, :] = up[:, c * LANES:(c + 1) * LANES]

    def conv(slab):
        cols = slice(slab * LANES, (slab + 1) * LANES)
        cur = up_ref[slab, CARRY_ROWS:CARRY_ROWS + ts, :]
        prev1 = up_ref[slab, CARRY_ROWS - 1:CARRY_ROWS - 1 + ts, :]
        prev2 = up_ref[slab, CARRY_ROWS - 2:CARRY_ROWS - 2 + ts, :]
        return (conv_b_ref[0, :, cols] + conv_w_ref[0, 0:1, cols] * prev2 + conv_w_ref[0, 1:2, cols] * prev1
                + conv_w_ref[0, 2:3, cols] * cur)

    for j in range(D_FF // FF_CHUNK):
        up_project(j * slabs_per_chunk)
        up_project(val_slab0 + j * slabs_per_chunk)
        for c in range(slabs_per_chunk):
            slab = j * slabs_per_chunk + c
            gate = conv(slab)
            val = conv(val_slab0 + slab)
            act = gate * (1.0 / (1.0 + jnp.exp(-gate))) * val
            act_ref[:, slab * LANES:(slab + 1) * LANES] = act.astype(BF16)

    up_ref[:, 0:CARRY_ROWS, :] = up_ref[:, ts:ts + CARRY_ROWS, :]
    y = h_ref[...] + jnp.dot(act_ref[...], w_down_ref[0], preferred_element_type=F32)
    if final_norm:
        y = _rmsnorm(y, fg_ref[...])
    o_ref[...] = y


def _ffn(h, g, w_up, conv_w, conv_b, w_down, final_g, *, layer, seq, final_norm):
    rows = h.shape[0]
    ts = FFN_ROW_TILE
    return pl.pallas_call(
        partial(_ffn_kernel, tiles_per_seq=seq // ts, final_norm=final_norm),
        out_shape=jax.ShapeDtypeStruct(h.shape, h.dtype),
        grid=(rows // ts,),
        in_specs=[
            pl.BlockSpec((ts, D_MODEL), lambda i: (i, 0)),
            _layer_spec((1, D_MODEL), layer),
            _layer_spec((D_MODEL, 2 * D_FF), layer),
            _layer_spec((CONV_WIDTH, 2 * D_FF), layer),
            _layer_spec((1, 2 * D_FF), layer),
            _layer_spec((D_FF, D_MODEL), layer),
            pl.BlockSpec((1, D_MODEL), lambda i: (0, 0), pipeline_mode=pl.Buffered(1)),
        ],
        out_specs=pl.BlockSpec((ts, D_MODEL), lambda i: (i, 0)),
        scratch_shapes=[
            pltpu.VMEM((ts, D_MODEL), BF16),
            pltpu.VMEM((2 * D_FF // LANES, CARRY_ROWS + ts, LANES), F32),
            pltpu.VMEM((ts, D_FF), BF16),
        ],
        compiler_params=pltpu.CompilerParams(
            dimension_semantics=("arbitrary",), vmem_limit_bytes=VMEM_LIMIT_BYTES),
        name="conv_ffn",
    )(h, g, w_up, conv_w, conv_b, w_down, final_g)


def _outfold_kernel(pool_w_ref, pool_scale_ref, w_out_ref, o_ref):
    for gi in range(len(POOL_WINDOWS)):
        rows = slice(gi * POOL_GROUP_DIM, (gi + 1) * POOL_GROUP_DIM)
        a = pool_w_ref[0, gi] * pool_scale_ref[0, :, rows]
        o_ref[0, rows, :] = jnp.dot(a, w_out_ref[0, rows, :], preferred_element_type=F32,
                                    precision=lax.Precision.HIGHEST).astype(BF16)
    o_ref[0, POOL_WIDTH:, :] = w_out_ref[0, POOL_WIDTH:, :].astype(BF16)


def _outfold(pool_w, pool_scale, w_out):
    depth = w_out.shape[0]
    return pl.pallas_call(
        _outfold_kernel,
        out_shape=jax.ShapeDtypeStruct(w_out.shape, BF16),
        grid=(depth,),
        in_specs=[
            pl.BlockSpec((1,) + pool_w.shape[1:], lambda l: (l, 0, 0, 0)),
            pl.BlockSpec((1, 1, POOL_WIDTH), lambda l: (l, 0, 0)),
            pl.BlockSpec((1, D_MODEL, D_MODEL), lambda l: (l, 0, 0)),
        ],
        out_specs=pl.BlockSpec((1, D_MODEL, D_MODEL), lambda l: (l, 0, 0)),
        compiler_params=pltpu.CompilerParams(
            dimension_semantics=("arbitrary",), vmem_limit_bytes=VMEM_LIMIT_BYTES),
        name="outfold",
    )(pool_w, pool_scale, w_out)


def kernel(x, mem, norm_mix_g, w_in, pool_w, pool_scale, sgu_g, sgu_w, sgu_b, w_out, norm_xattn_g, mem_norm_g,
           wq, wk, wv, wo, norm_ffn_g, w_up, conv_w, conv_b, w_down, final_norm_g):
    batch, seq, d_model = x.shape
    depth = w_in.shape[0]
    assert d_model == D_MODEL and seq % (ROW_TILE * XATTN_SUBTILES) == 0 and seq % (ROW_TILE * MIXER_SUBTILES) == 0
    assert seq % FFN_ROW_TILE == 0 and ROW_TILE % CHUNK == 0 and D_FF % FF_CHUNK == 0

    def bf(w):
        return w.astype(BF16)

    def row(p):
        return p[:, None, :]

    wqk, vo = _memkv(mem, row(mem_norm_g), bf(wk), bf(wv), bf(wq), bf(wo))
    w_out_folded = _outfold(pool_w, row(pool_scale), w_out)
    sgu_b_full = jnp.repeat(jnp.swapaxes(sgu_b, 1, 2), SGU_HEAD_DIM, axis=2)
    w_in_b, sgu_w_b, w_up_b, w_down_b = bf(w_in), bf(sgu_w), bf(w_up), bf(w_down)

    h = x.reshape(batch * seq, d_model)
    for l in range(depth):
        h = _mixer(h, row(norm_mix_g), w_in_b, row(sgu_g), sgu_w_b, sgu_b_full, w_out_folded, layer=l, seq=seq)
        h = _xattn(h, row(norm_xattn_g), wqk, vo, layer=l, seq=seq)
        h = _ffn(h, row(norm_ffn_g), w_up_b, conv_w, row(conv_b), w_down_b, final_norm_g[None],
                 layer=l, seq=seq, final_norm=(l == depth - 1))
    return h.reshape(batch, seq, d_model)
```

```python
from functools import partial

import jax
import jax.numpy as jnp
from jax import lax
from jax.experimental import pallas as pl
from jax.experimental.pallas import tpu as pltpu

D_MODEL = 1024
N_MEM = 256
POOL_WIDTH = 512
POOL_WINDOWS = (2, 4, 8, 16)
POOL_GROUP_DIM = 128
POOL_LEVEL_BUFFERS = 6
POOL_HALO = 32
SGU_WIDTH = 512
SGU_HEADS = 4
SGU_HEAD_DIM = 128
CHUNK = 128
D_IN_PROJ = POOL_WIDTH + 2 * SGU_WIDTH
XATTN_HEADS = 4
XATTN_HEAD_DIM = 256
D_FF = 2816
CONV_WIDTH = 3
EPS = 1e-6

ROW_TILE = 512
MIXER_SUBTILES = 2
XATTN_SUBTILES = 4
FFN_SUBTILES = 2
FF_CHUNK = 256
LANES = 128
CARRY_ROWS = 8
VMEM_LIMIT_BYTES = 56 * 1024 * 1024

BF16 = jnp.bfloat16
F32 = jnp.float32


def _rmsnorm(x, g):
    return x * lax.rsqrt(jnp.mean(x * x, axis=-1, keepdims=True) + EPS) * g


def _gelu_exact(x):
    return 0.5 * x * (1.0 + lax.erf(x * (0.5 ** 0.5)))


def _layer_spec(tail, layer):
    return pl.BlockSpec((1,) + tail, lambda i: (layer,) + (0,) * len(tail), pipeline_mode=pl.Buffered(1))


def _mixer_kernel(h_ref, g_ref, w_in_ref, sgu_g_ref, sgu_w_ref, sgu_b_ref, w_out_ref, o_ref,
                  xn_ref, u_ref, vn_ref, pbuf_ref, ptmp_ref, ycat_ref, *, tiles_per_seq):
    ts = h_ref.shape[0]
    sub = ts // MIXER_SUBTILES
    i = pl.program_id(0)
    tile_in_seq = i % tiles_per_seq
    u_cols = slice(POOL_WIDTH, POOL_WIDTH + SGU_WIDTH)
    v_cols = slice(POOL_WIDTH + SGU_WIDTH, D_IN_PROJ)
    tri = (lax.broadcasted_iota(jnp.int32, (CHUNK, CHUNK), 0)
           >= lax.broadcasted_iota(jnp.int32, (CHUNK, CHUNK), 1))

    @pl.when(tile_in_seq == 0)
    def _():
        pbuf_ref[:, 0:POOL_HALO, :] = jnp.zeros((len(POOL_WINDOWS), POOL_HALO, POOL_GROUP_DIM), F32)

    for r in range(MIXER_SUBTILES):
        rows = slice(r * sub, (r + 1) * sub)
        xn_ref[rows, :] = _rmsnorm(h_ref[rows, :], g_ref[0]).astype(BF16)

        v = _gelu_exact(jnp.dot(xn_ref[rows, :], w_in_ref[0, :, v_cols], preferred_element_type=F32))
        mu = jnp.mean(v, axis=-1, keepdims=True)
        vc = v - mu
        vn_ref[rows, :] = (vc * lax.rsqrt(jnp.mean(vc * vc, axis=-1, keepdims=True) + EPS)
                           * sgu_g_ref[0]).astype(BF16)
        p_all = jnp.dot(xn_ref[rows, :], w_in_ref[0, :, :POOL_WIDTH], preferred_element_type=F32)
        for gi in range(len(POOL_WINDOWS)):
            pbuf_ref[gi, POOL_HALO + r * sub:POOL_HALO + (r + 1) * sub, :] = (
                p_all[:, gi * POOL_GROUP_DIM:(gi + 1) * POOL_GROUP_DIM])
        u_ref[rows, :] = _gelu_exact(jnp.dot(xn_ref[rows, :], w_in_ref[0, :, u_cols], preferred_element_type=F32))

    for r in range(MIXER_SUBTILES):
        rows = slice(r * sub, (r + 1) * sub)

        base = r * sub
        n_tmp = 0
        for gi, win in enumerate(POOL_WINDOWS):
            lanes = slice(gi * POOL_GROUP_DIM, (gi + 1) * POOL_GROUP_DIM)

            def read(lo, n, gi=gi):
                return pbuf_ref[gi, base + lo:base + lo + n, :]

            span, first = 1, 0
            while 2 * span < win:
                first += 8
                n = POOL_HALO + sub - first
                tmp = ptmp_ref.at[r, n_tmp]
                n_tmp += 1
                tmp[first:first + n, :] = read(first, n) + read(first - span, n)

                def read(lo, n, tmp=tmp):
                    return tmp[lo:lo + n, :]

                span *= 2
            s = read(POOL_HALO, sub) + read(POOL_HALO - span, sub)
            p = pbuf_ref[gi, base + POOL_HALO:base + POOL_HALO + sub, :]
            if r == 0:
                nh = POOL_WINDOWS[-1]
                t1 = lax.broadcasted_iota(jnp.int32, (nh, 1), 0) + 1
                count = jnp.where(tile_in_seq == 0, jnp.minimum(t1, win), win)
                head = s[:nh, :] * (1.0 / count.astype(F32)) - p[:nh, :]
                ycat_ref[0:nh, lanes] = head.astype(BF16)
                ycat_ref[nh:sub, lanes] = (s[nh:, :] * (1.0 / win) - p[nh:, :]).astype(BF16)
            else:
                ycat_ref[rows, lanes] = (s * (1.0 / win) - p).astype(BF16)

        for hd in range(SGU_HEADS):
            lanes = slice(hd * SGU_HEAD_DIM, (hd + 1) * SGU_HEAD_DIM)
            w_masked = jnp.where(tri, sgu_w_ref[0, hd], jnp.zeros((), BF16))
            bias = sgu_b_ref[0, :, lanes]
            for c in range(sub // CHUNK):
                crows = slice(r * sub + c * CHUNK, r * sub + (c + 1) * CHUNK)
                z = jnp.dot(w_masked, vn_ref[crows, lanes], preferred_element_type=F32) + bias
                ycat_ref[crows, POOL_WIDTH + hd * SGU_HEAD_DIM:POOL_WIDTH + (hd + 1) * SGU_HEAD_DIM] = (
                    u_ref[crows, lanes] * z).astype(BF16)

        o_ref[rows, :] = (h_ref[rows, :]
                          + jnp.dot(ycat_ref[rows, :POOL_WIDTH], w_out_ref[0, :POOL_WIDTH, :],
                                    preferred_element_type=F32)
                          + jnp.dot(ycat_ref[rows, POOL_WIDTH:], w_out_ref[0, POOL_WIDTH:, :],
                                    preferred_element_type=F32))

    pbuf_ref[:, 0:POOL_HALO, :] = pbuf_ref[:, ts:, :]


def _mixer(h, g, w_in, sgu_g, sgu_w, sgu_b_full, w_out, *, layer, seq):
    rows = h.shape[0]
    ts = ROW_TILE * MIXER_SUBTILES
    return pl.pallas_call(
        partial(_mixer_kernel, tiles_per_seq=seq // ts),
        out_shape=jax.ShapeDtypeStruct(h.shape, h.dtype),
        grid=(rows // ts,),
        in_specs=[
            pl.BlockSpec((ts, D_MODEL), lambda i: (i, 0)),
            _layer_spec((1, D_MODEL), layer),
            _layer_spec((D_MODEL, D_IN_PROJ), layer),
            _layer_spec((1, SGU_WIDTH), layer),
            _layer_spec((SGU_HEADS, CHUNK, CHUNK), layer),
            _layer_spec((CHUNK, SGU_WIDTH), layer),
            _layer_spec((D_MODEL, D_MODEL), layer),
        ],
        out_specs=pl.BlockSpec((ts, D_MODEL), lambda i: (i, 0)),
        scratch_shapes=[
            pltpu.VMEM((ts, D_MODEL), BF16),
            pltpu.VMEM((ts, SGU_WIDTH), F32),
            pltpu.VMEM((ts, SGU_WIDTH), BF16),
            pltpu.VMEM((len(POOL_WINDOWS), POOL_HALO + ts, POOL_GROUP_DIM), F32),
            pltpu.VMEM((MIXER_SUBTILES, POOL_LEVEL_BUFFERS, POOL_HALO + ts // MIXER_SUBTILES, POOL_GROUP_DIM), F32),
            pltpu.VMEM((ts, D_MODEL), BF16),
        ],
        compiler_params=pltpu.CompilerParams(
            dimension_semantics=("arbitrary",), vmem_limit_bytes=VMEM_LIMIT_BYTES),
        name="mixer",
    )(h, g, w_in, sgu_g, sgu_w, sgu_b_full, w_out)


def _memkv_kernel(mem_ref, g_ref, wk_ref, wv_ref, wq_ref, wo_ref, wqk_ref, vo_ref):
    memn = _rmsnorm(mem_ref[0], g_ref[0]).astype(BF16)
    k = jnp.dot(memn, wk_ref[0], preferred_element_type=F32).astype(BF16)
    v = jnp.dot(memn, wv_ref[0], preferred_element_type=F32).astype(BF16)
    for hd in range(XATTN_HEADS):
        cols = slice(hd * XATTN_HEAD_DIM, (hd + 1) * XATTN_HEAD_DIM)
        qk = lax.dot_general(wq_ref[0, :, cols], k[:, cols], (((1,), (1,)), ((), ())),
                             preferred_element_type=F32)
        wqk_ref[0, 0, :, hd * N_MEM:(hd + 1) * N_MEM] = (qk * (XATTN_HEAD_DIM ** -0.5)).astype(BF16)
        vo_ref[0, 0, hd * N_MEM:(hd + 1) * N_MEM, :] = jnp.dot(
            v[:, cols], wo_ref[0, cols, :], preferred_element_type=F32).astype(BF16)


def _memkv(mem, mem_g, wk, wv, wq, wo):
    depth = wk.shape[0]
    batch = mem.shape[0]
    w_spec = pl.BlockSpec((1, D_MODEL, D_MODEL), lambda l, b: (l, 0, 0))
    return pl.pallas_call(
        _memkv_kernel,
        out_shape=(jax.ShapeDtypeStruct((depth, batch, D_MODEL, XATTN_HEADS * N_MEM), BF16),
                   jax.ShapeDtypeStruct((depth, batch, XATTN_HEADS * N_MEM, D_MODEL), BF16)),
        grid=(depth, batch),
        in_specs=[
            pl.BlockSpec((1, N_MEM, D_MODEL), lambda l, b: (b, 0, 0)),
            pl.BlockSpec((1, 1, D_MODEL), lambda l, b: (l, 0, 0)),
            w_spec, w_spec, w_spec, w_spec,
        ],
        out_specs=(pl.BlockSpec((1, 1, D_MODEL, XATTN_HEADS * N_MEM), lambda l, b: (l, b, 0, 0)),
                   pl.BlockSpec((1, 1, XATTN_HEADS * N_MEM, D_MODEL), lambda l, b: (l, b, 0, 0))),
        compiler_params=pltpu.CompilerParams(
            dimension_semantics=("arbitrary", "arbitrary"), vmem_limit_bytes=VMEM_LIMIT_BYTES),
        name="memkv",
    )(mem, mem_g, wk, wv, wq, wo)


def _xattn_kernel(h_ref, g_ref, wqk_ref, vo_ref, o_ref, pr_ref):
    sub = h_ref.shape[0] // XATTN_SUBTILES

    def scores(r):
        rows = slice(r * sub, (r + 1) * sub)
        xn = _rmsnorm(h_ref[rows, :], g_ref[0]).astype(BF16)
        s = jnp.dot(xn, wqk_ref[0, 0], preferred_element_type=F32)
        for hd in range(XATTN_HEADS):
            cols = slice(hd * N_MEM, (hd + 1) * N_MEM)
            sh = s[:, cols]
            e = jnp.exp(sh - jnp.max(sh, axis=-1, keepdims=True))
            pr_ref[rows, cols] = (e / jnp.sum(e, axis=-1, keepdims=True)).astype(BF16)

    def attend(r):
        rows = slice(r * sub, (r + 1) * sub)
        o_ref[rows, :] = h_ref[rows, :] + jnp.dot(pr_ref[rows, :], vo_ref[0, 0], preferred_element_type=F32)

    scores(0)
    for r in range(XATTN_SUBTILES):
        if r + 1 < XATTN_SUBTILES:
            scores(r + 1)
        attend(r)


def _xattn(h, g, wqk, vo, *, layer, seq):
    rows = h.shape[0]
    ts = ROW_TILE * XATTN_SUBTILES
    tiles_per_seq = seq // ts
    kv_spec = pl.BlockSpec((1, 1) + wqk.shape[2:], lambda i: (layer, i // tiles_per_seq, 0, 0))
    return pl.pallas_call(
        _xattn_kernel,
        out_shape=jax.ShapeDtypeStruct(h.shape, h.dtype),
        grid=(rows // ts,),
        in_specs=[
            pl.BlockSpec((ts, D_MODEL), lambda i: (i, 0)),
            _layer_spec((1, D_MODEL), layer),
            kv_spec,
            kv_spec,
        ],
        out_specs=pl.BlockSpec((ts, D_MODEL), lambda i: (i, 0)),
        scratch_shapes=[pltpu.VMEM((ts, XATTN_HEADS * N_MEM), BF16)],
        compiler_params=pltpu.CompilerParams(
            dimension_semantics=("arbitrary",), vmem_limit_bytes=VMEM_LIMIT_BYTES),
        name="xattn",
    )(h, g, wqk, vo)


def _ffn_kernel(h_ref, g_ref, w_up_ref, conv_w_ref, conv_b_ref, w_down_ref, fg_ref, o_ref,
                xn_ref, up_ref, act_ref, *, tiles_per_seq, final_norm):
    sub = h_ref.shape[0] // FFN_SUBTILES
    i = pl.program_id(0)
    slabs_per_chunk = FF_CHUNK // LANES
    val_slab0 = D_FF // LANES

    @pl.when(i % tiles_per_seq == 0)
    def _():
        up_ref[:, 0:CARRY_ROWS, :] = jnp.zeros((up_ref.shape[0], CARRY_ROWS, LANES), F32)

    def norm(r):
        xn_ref[...] = _rmsnorm(h_ref[r * sub:(r + 1) * sub, :], g_ref[0]).astype(BF16)

    def up_project(first_slab):
        c0 = first_slab * LANES
        up = jnp.dot(xn_ref[...], w_up_ref[0, :, c0:c0 + FF_CHUNK], preferred_element_type=F32)
        for c in range(slabs_per_chunk):
            up_ref[first_slab + c, CARRY_ROWS:CARRY_ROWS + sub, :] = up[:, c * LANES:(c + 1) * LANES]

    def conv(slab):
        cols = slice(slab * LANES, (slab + 1) * LANES)
        cur = up_ref[slab, CARRY_ROWS:CARRY_ROWS + sub, :]
        prev1 = up_ref[slab, CARRY_ROWS - 1:CARRY_ROWS - 1 + sub, :]
        prev2 = up_ref[slab, CARRY_ROWS - 2:CARRY_ROWS - 2 + sub, :]
        return (conv_b_ref[0, :, cols] + conv_w_ref[0, 0:1, cols] * prev2 + conv_w_ref[0, 1:2, cols] * prev1
                + conv_w_ref[0, 2:3, cols] * cur)

    def hidden():
        for j in range(D_FF // FF_CHUNK):
            up_project(j * slabs_per_chunk)
            up_project(val_slab0 + j * slabs_per_chunk)
            for c in range(slabs_per_chunk):
                slab = j * slabs_per_chunk + c
                gate = conv(slab)
                val = conv(val_slab0 + slab)
                act = gate * (1.0 / (1.0 + jnp.exp(-gate))) * val
                act_ref[:, slab * LANES:(slab + 1) * LANES] = act.astype(BF16)
        up_ref[:, 0:CARRY_ROWS, :] = up_ref[:, sub:sub + CARRY_ROWS, :]

    def project_down(r):
        rows = slice(r * sub, (r + 1) * sub)
        y = h_ref[rows, :]
        for j in range(D_FF // FF_CHUNK):
            ks = slice(j * FF_CHUNK, (j + 1) * FF_CHUNK)
            y = y + jnp.dot(act_ref[:, ks], w_down_ref[0, ks, :], preferred_element_type=F32)
        if final_norm:
            y = _rmsnorm(y, fg_ref[...])
        o_ref[rows, :] = y

    norm(0)
    for r in range(FFN_SUBTILES):
        hidden()
        if r + 1 < FFN_SUBTILES:
            norm(r + 1)
        project_down(r)


def _ffn(h, g, w_up, conv_w, conv_b, w_down, final_g, *, layer, seq, final_norm):
    rows = h.shape[0]
    ts = ROW_TILE * FFN_SUBTILES
    return pl.pallas_call(
        partial(_ffn_kernel, tiles_per_seq=seq // ts, final_norm=final_norm),
        out_shape=jax.ShapeDtypeStruct(h.shape, h.dtype),
        grid=(rows // ts,),
        in_specs=[
            pl.BlockSpec((ts, D_MODEL), lambda i: (i, 0)),
            _layer_spec((1, D_MODEL), layer),
            _layer_spec((D_MODEL, 2 * D_FF), layer),
            _layer_spec((CONV_WIDTH, 2 * D_FF), layer),
            _layer_spec((1, 2 * D_FF), layer),
            _layer_spec((D_FF, D_MODEL), layer),
            pl.BlockSpec((1, D_MODEL), lambda i: (0, 0), pipeline_mode=pl.Buffered(1)),
        ],
        out_specs=pl.BlockSpec((ts, D_MODEL), lambda i: (i, 0)),
        scratch_shapes=[
            pltpu.VMEM((ROW_TILE, D_MODEL), BF16),
            pltpu.VMEM((2 * D_FF // LANES, CARRY_ROWS + ROW_TILE, LANES), F32),
            pltpu.VMEM((ROW_TILE, D_FF), BF16),
        ],
        compiler_params=pltpu.CompilerParams(
            dimension_semantics=("arbitrary",), vmem_limit_bytes=VMEM_LIMIT_BYTES),
        name="conv_ffn",
    )(h, g, w_up, conv_w, conv_b, w_down, final_g)


def _outfold_kernel(pool_w_ref, pool_scale_ref, w_out_ref, o_ref):
    for gi in range(len(POOL_WINDOWS)):
        rows = slice(gi * POOL_GROUP_DIM, (gi + 1) * POOL_GROUP_DIM)
        a = pool_w_ref[0, gi] * pool_scale_ref[0, :, rows]
        o_ref[0, rows, :] = jnp.dot(a, w_out_ref[0, rows, :], preferred_element_type=F32,
                                    precision=lax.Precision.HIGHEST).astype(BF16)
    o_ref[0, POOL_WIDTH:, :] = w_out_ref[0, POOL_WIDTH:, :].astype(BF16)


def _outfold(pool_w, pool_scale, w_out):
    depth = w_out.shape[0]
    return pl.pallas_call(
        _outfold_kernel,
        out_shape=jax.ShapeDtypeStruct(w_out.shape, BF16),
        grid=(depth,),
        in_specs=[
            pl.BlockSpec((1,) + pool_w.shape[1:], lambda l: (l, 0, 0, 0)),
            pl.BlockSpec((1, 1, POOL_WIDTH), lambda l: (l, 0, 0)),
            pl.BlockSpec((1, D_MODEL, D_MODEL), lambda l: (l, 0, 0)),
        ],
        out_specs=pl.BlockSpec((1, D_MODEL, D_MODEL), lambda l: (l, 0, 0)),
        compiler_params=pltpu.CompilerParams(
            dimension_semantics=("arbitrary",), vmem_limit_bytes=VMEM_LIMIT_BYTES),
        name="outfold",
    )(pool_w, pool_scale, w_out)


def kernel(x, mem, norm_mix_g, w_in, pool_w, pool_scale, sgu_g, sgu_w, sgu_b, w_out, norm_xattn_g, mem_norm_g,
           wq, wk, wv, wo, norm_ffn_g, w_up, conv_w, conv_b, w_down, final_norm_g):
    batch, seq, d_model = x.shape
    depth = w_in.shape[0]
    assert d_model == D_MODEL and seq % (ROW_TILE * XATTN_SUBTILES) == 0 and seq % (ROW_TILE * MIXER_SUBTILES) == 0
    assert seq % (ROW_TILE * FFN_SUBTILES) == 0 and ROW_TILE % CHUNK == 0 and D_FF % FF_CHUNK == 0

    def bf(w):
        return w.astype(BF16)

    def row(p):
        return p[:, None, :]

    wqk, vo = _memkv(mem, row(mem_norm_g), bf(wk), bf(wv), bf(wq), bf(wo))
    w_out_folded = _outfold(pool_w, row(pool_scale), w_out)
    sgu_b_full = jnp.repeat(jnp.swapaxes(sgu_b, 1, 2), SGU_HEAD_DIM, axis=2)
    w_in_b, sgu_w_b, w_up_b, w_down_b = bf(w_in), bf(sgu_w), bf(w_up), bf(w_down)

    h = x.reshape(batch * seq, d_model)
    for l in range(depth):
        h = _mixer(h, row(norm_mix_g), w_in_b, row(sgu_g), sgu_w_b, sgu_b_full, w_out_folded, layer=l, seq=seq)
        h = _xattn(h, row(norm_xattn_g), wqk, vo, layer=l, seq=seq)
        h = _ffn(h, row(norm_ffn_g), w_up_b, conv_w, row(conv_b), w_down_b, final_norm_g[None],
                 layer=l, seq=seq, final_norm=(l == depth - 1))
    return h.reshape(batch, seq, d_model)
```

```python
from functools import partial

import jax
import jax.numpy as jnp
from jax import lax
from jax.experimental import pallas as pl
from jax.experimental.pallas import tpu as pltpu

D_MODEL = 1024
N_MEM = 256
POOL_WIDTH = 512
POOL_WINDOWS = (2, 4, 8, 16)
POOL_GROUP_DIM = 128
POOL_LEVEL_BUFFERS = 6
POOL_HALO = 32
SGU_WIDTH = 512
SGU_HEADS = 4
SGU_HEAD_DIM = 128
CHUNK = 128
D_IN_PROJ = POOL_WIDTH + 2 * SGU_WIDTH
XATTN_HEADS = 4
XATTN_HEAD_DIM = 256
D_FF = 2816
CONV_WIDTH = 3
EPS = 1e-6

ROW_TILE = 512
MIXER_SUBTILES = 2
XATTN_SUBTILES = 4
FFN_ROW_TILE = 512
FF_CHUNK = 256
LANES = 128
CARRY_ROWS = 8
VMEM_LIMIT_BYTES = 56 * 1024 * 1024

BF16 = jnp.bfloat16
F32 = jnp.float32


def _rmsnorm(x, g):
    return x * lax.rsqrt(jnp.mean(x * x, axis=-1, keepdims=True) + EPS) * g


def _gelu_exact(x):
    return 0.5 * x * (1.0 + lax.erf(x * (0.5 ** 0.5)))


def _layer_spec(tail, layer):
    return pl.BlockSpec((1,) + tail, lambda i: (layer,) + (0,) * len(tail), pipeline_mode=pl.Buffered(1))


def _mixer_kernel(h_ref, g_ref, w_in_ref, sgu_g_ref, sgu_w_ref, sgu_b_ref, w_out_ref, o_ref,
                  xn_ref, u_ref, vn_ref, pbuf_ref, ptmp_ref, ycat_ref, *, tiles_per_seq):
    ts = h_ref.shape[0]
    sub = ts // MIXER_SUBTILES
    i = pl.program_id(0)
    tile_in_seq = i % tiles_per_seq
    u_cols = slice(POOL_WIDTH, POOL_WIDTH + SGU_WIDTH)
    v_cols = slice(POOL_WIDTH + SGU_WIDTH, D_IN_PROJ)
    tri = (lax.broadcasted_iota(jnp.int32, (CHUNK, CHUNK), 0)
           >= lax.broadcasted_iota(jnp.int32, (CHUNK, CHUNK), 1))

    @pl.when(tile_in_seq == 0)
    def _():
        pbuf_ref[:, 0:POOL_HALO, :] = jnp.zeros((len(POOL_WINDOWS), POOL_HALO, POOL_GROUP_DIM), F32)

    for r in range(MIXER_SUBTILES):
        rows = slice(r * sub, (r + 1) * sub)
        xn_ref[rows, :] = _rmsnorm(h_ref[rows, :], g_ref[0]).astype(BF16)

        v = _gelu_exact(jnp.dot(xn_ref[rows, :], w_in_ref[0, :, v_cols], preferred_element_type=F32))
        mu = jnp.mean(v, axis=-1, keepdims=True)
        vc = v - mu
        vn_ref[rows, :] = (vc * lax.rsqrt(jnp.mean(vc * vc, axis=-1, keepdims=True) + EPS)
                           * sgu_g_ref[0]).astype(BF16)
        p_all = jnp.dot(xn_ref[rows, :], w_in_ref[0, :, :POOL_WIDTH], preferred_element_type=F32)
        for gi in range(len(POOL_WINDOWS)):
            pbuf_ref[gi, POOL_HALO + r * sub:POOL_HALO + (r + 1) * sub, :] = (
                p_all[:, gi * POOL_GROUP_DIM:(gi + 1) * POOL_GROUP_DIM])
        u_ref[rows, :] = _gelu_exact(jnp.dot(xn_ref[rows, :], w_in_ref[0, :, u_cols], preferred_element_type=F32))

    for r in range(MIXER_SUBTILES):
        rows = slice(r * sub, (r + 1) * sub)

        base = r * sub
        n_tmp = 0
        for gi, win in enumerate(POOL_WINDOWS):
            lanes = slice(gi * POOL_GROUP_DIM, (gi + 1) * POOL_GROUP_DIM)

            def read(lo, n, gi=gi):
                return pbuf_ref[gi, base + lo:base + lo + n, :]

            span, first = 1, 0
            while 2 * span < win:
                first += 8
                n = POOL_HALO + sub - first
                tmp = ptmp_ref.at[r, n_tmp]
                n_tmp += 1
                tmp[first:first + n, :] = read(first, n) + read(first - span, n)

                def read(lo, n, tmp=tmp):
                    return tmp[lo:lo + n, :]

                span *= 2
            s = read(POOL_HALO, sub) + read(POOL_HALO - span, sub)
            p = pbuf_ref[gi, base + POOL_HALO:base + POOL_HALO + sub, :]
            if r == 0:
                nh = POOL_WINDOWS[-1]
                t1 = lax.broadcasted_iota(jnp.int32, (nh, 1), 0) + 1
                count = jnp.where(tile_in_seq == 0, jnp.minimum(t1, win), win)
                head = s[:nh, :] * (1.0 / count.astype(F32)) - p[:nh, :]
                ycat_ref[0:nh, lanes] = head.astype(BF16)
                ycat_ref[nh:sub, lanes] = (s[nh:, :] * (1.0 / win) - p[nh:, :]).astype(BF16)
            else:
                ycat_ref[rows, lanes] = (s * (1.0 / win) - p).astype(BF16)

        for hd in range(SGU_HEADS):
            lanes = slice(hd * SGU_HEAD_DIM, (hd + 1) * SGU_HEAD_DIM)
            w_masked = jnp.where(tri, sgu_w_ref[0, hd], jnp.zeros((), BF16))
            bias = sgu_b_ref[0, :, lanes]
            for c in range(sub // CHUNK):
                crows = slice(r * sub + c * CHUNK, r * sub + (c + 1) * CHUNK)
                z = jnp.dot(w_masked, vn_ref[crows, lanes], preferred_element_type=F32) + bias
                ycat_ref[crows, POOL_WIDTH + hd * SGU_HEAD_DIM:POOL_WIDTH + (hd + 1) * SGU_HEAD_DIM] = (
                    u_ref[crows, lanes] * z).astype(BF16)

        o_ref[rows, :] = h_ref[rows, :] + jnp.dot(ycat_ref[rows, :], w_out_ref[0], preferred_element_type=F32)

    pbuf_ref[:, 0:POOL_HALO, :] = pbuf_ref[:, ts:, :]


def _mixer(h, g, w_in, sgu_g, sgu_w, sgu_b_full, w_out, *, layer, seq):
    rows = h.shape[0]
    ts = ROW_TILE * MIXER_SUBTILES
    return pl.pallas_call(
        partial(_mixer_kernel, tiles_per_seq=seq // ts),
        out_shape=jax.ShapeDtypeStruct(h.shape, h.dtype),
        grid=(rows // ts,),
        in_specs=[
            pl.BlockSpec((ts, D_MODEL), lambda i: (i, 0)),
            _layer_spec((1, D_MODEL), layer),
            _layer_spec((D_MODEL, D_IN_PROJ), layer),
            _layer_spec((1, SGU_WIDTH), layer),
            _layer_spec((SGU_HEADS, CHUNK, CHUNK), layer),
            _layer_spec((CHUNK, SGU_WIDTH), layer),
            _layer_spec((D_MODEL, D_MODEL), layer),
        ],
        out_specs=pl.BlockSpec((ts, D_MODEL), lambda i: (i, 0)),
        scratch_shapes=[
            pltpu.VMEM((ts, D_MODEL), BF16),
            pltpu.VMEM((ts, SGU_WIDTH), F32),
            pltpu.VMEM((ts, SGU_WIDTH), BF16),
            pltpu.VMEM((len(POOL_WINDOWS), POOL_HALO + ts, POOL_GROUP_DIM), F32),
            pltpu.VMEM((MIXER_SUBTILES, POOL_LEVEL_BUFFERS, POOL_HALO + ts // MIXER_SUBTILES, POOL_GROUP_DIM), F32),
            pltpu.VMEM((ts, D_MODEL), BF16),
        ],
        compiler_params=pltpu.CompilerParams(
            dimension_semantics=("arbitrary",), vmem_limit_bytes=VMEM_LIMIT_BYTES),
        name="mixer",
    )(h, g, w_in, sgu_g, sgu_w, sgu_b_full, w_out)


def _memkv_kernel(mem_ref, g_ref, wk_ref, wv_ref, wq_ref, wo_ref, wqk_ref, vo_ref):
    memn = _rmsnorm(mem_ref[0], g_ref[0]).astype(BF16)
    k = jnp.dot(memn, wk_ref[0], preferred_element_type=F32).astype(BF16)
    v = jnp.dot(memn, wv_ref[0], preferred_element_type=F32).astype(BF16)
    for hd in range(XATTN_HEADS):
        cols = slice(hd * XATTN_HEAD_DIM, (hd + 1) * XATTN_HEAD_DIM)
        qk = lax.dot_general(wq_ref[0, :, cols], k[:, cols], (((1,), (1,)), ((), ())),
                             preferred_element_type=F32)
        wqk_ref[0, 0, :, hd * N_MEM:(hd + 1) * N_MEM] = (qk * (XATTN_HEAD_DIM ** -0.5)).astype(BF16)
        vo_ref[0, 0, hd * N_MEM:(hd + 1) * N_MEM, :] = jnp.dot(
            v[:, cols], wo_ref[0, cols, :], preferred_element_type=F32).astype(BF16)


def _memkv(mem, mem_g, wk, wv, wq, wo):
    depth = wk.shape[0]
    batch = mem.shape[0]
    w_spec = pl.BlockSpec((1, D_MODEL, D_MODEL), lambda l, b: (l, 0, 0))
    return pl.pallas_call(
        _memkv_kernel,
        out_shape=(jax.ShapeDtypeStruct((depth, batch, D_MODEL, XATTN_HEADS * N_MEM), BF16),
                   jax.ShapeDtypeStruct((depth, batch, XATTN_HEADS * N_MEM, D_MODEL), BF16)),
        grid=(depth, batch),
        in_specs=[
            pl.BlockSpec((1, N_MEM, D_MODEL), lambda l, b: (b, 0, 0)),
            pl.BlockSpec((1, 1, D_MODEL), lambda l, b: (l, 0, 0)),
            w_spec, w_spec, w_spec, w_spec,
        ],
        out_specs=(pl.BlockSpec((1, 1, D_MODEL, XATTN_HEADS * N_MEM), lambda l, b: (l, b, 0, 0)),
                   pl.BlockSpec((1, 1, XATTN_HEADS * N_MEM, D_MODEL), lambda l, b: (l, b, 0, 0))),
        compiler_params=pltpu.CompilerParams(
            dimension_semantics=("arbitrary", "arbitrary"), vmem_limit_bytes=VMEM_LIMIT_BYTES),
        name="memkv",
    )(mem, mem_g, wk, wv, wq, wo)


def _xattn_kernel(h_ref, g_ref, wqk_ref, vo_ref, o_ref, pr_ref):
    sub = h_ref.shape[0] // XATTN_SUBTILES

    def scores(r):
        rows = slice(r * sub, (r + 1) * sub)
        xn = _rmsnorm(h_ref[rows, :], g_ref[0]).astype(BF16)
        s = jnp.dot(xn, wqk_ref[0, 0], preferred_element_type=F32)
        for hd in range(XATTN_HEADS):
            cols = slice(hd * N_MEM, (hd + 1) * N_MEM)
            sh = s[:, cols]
            e = jnp.exp(sh - jnp.max(sh, axis=-1, keepdims=True))
            pr_ref[rows, cols] = (e / jnp.sum(e, axis=-1, keepdims=True)).astype(BF16)

    def attend(r):
        rows = slice(r * sub, (r + 1) * sub)
        o_ref[rows, :] = h_ref[rows, :] + jnp.dot(pr_ref[rows, :], vo_ref[0, 0], preferred_element_type=F32)

    scores(0)
    for r in range(XATTN_SUBTILES):
        if r + 1 < XATTN_SUBTILES:
            scores(r + 1)
        attend(r)


def _xattn(h, g, wqk, vo, *, layer, seq):
    rows = h.shape[0]
    ts = ROW_TILE * XATTN_SUBTILES
    tiles_per_seq = seq // ts
    kv_spec = pl.BlockSpec((1, 1) + wqk.shape[2:], lambda i: (layer, i // tiles_per_seq, 0, 0))
    return pl.pallas_call(
        _xattn_kernel,
        out_shape=jax.ShapeDtypeStruct(h.shape, h.dtype),
        grid=(rows // ts,),
        in_specs=[
            pl.BlockSpec((ts, D_MODEL), lambda i: (i, 0)),
            _layer_spec((1, D_MODEL), layer),
            kv_spec,
            kv_spec,
        ],
        out_specs=pl.BlockSpec((ts, D_MODEL), lambda i: (i, 0)),
        scratch_shapes=[pltpu.VMEM((ts, XATTN_HEADS * N_MEM), BF16)],
        compiler_params=pltpu.CompilerParams(
            dimension_semantics=("arbitrary",), vmem_limit_bytes=VMEM_LIMIT_BYTES),
        name="xattn",
    )(h, g, wqk, vo)


def _ffn_kernel(h_ref, g_ref, w_up_ref, conv_w_ref, conv_b_ref, w_down_ref, fg_ref, o_ref,
                xn_ref, up_ref, act_ref, *, tiles_per_seq, final_norm):
    ts = h_ref.shape[0]
    i = pl.program_id(0)
    slabs_per_chunk = FF_CHUNK // LANES
    val_slab0 = D_FF // LANES
    halves = (slice(0, ts // 2), slice(ts // 2, ts))

    @pl.when(i % tiles_per_seq == 0)
    def _():
        up_ref[:, 0:CARRY_ROWS, :] = jnp.zeros((up_ref.shape[0], CARRY_ROWS, LANES), F32)

    def up_project(first_slab):
        c0 = first_slab * LANES
        up = jnp.dot(xn_ref[...], w_up_ref[0, :, c0:c0 + FF_CHUNK], preferred_element_type=F32)
        for c in range(slabs_per_chunk):
            up_ref[first_slab + c, CARRY_ROWS:CARRY_ROWS + ts, :] = up[:, c * LANES:(c + 1) * LANES]

    def conv(slab):
        cols = slice(slab * LANES, (slab + 1) * LANES)
        cur = up_ref[slab, CARRY_ROWS:CARRY_ROWS + ts, :]
        prev1 = up_ref[slab, CARRY_ROWS - 1:CARRY_ROWS - 1 + ts, :]
        prev2 = up_ref[slab, CARRY_ROWS - 2:CARRY_ROWS - 2 + ts, :]
        return (conv_b_ref[0, :, cols] + conv_w_ref[0, 0:1, cols] * prev2 + conv_w_ref[0, 1:2, cols] * prev1
                + conv_w_ref[0, 2:3, cols] * cur)

    xn_ref[...] = _rmsnorm(h_ref[...], g_ref[0]).astype(BF16)
    for j in range(D_FF // FF_CHUNK):
        up_project(j * slabs_per_chunk)
        up_project(val_slab0 + j * slabs_per_chunk)
        for c in range(slabs_per_chunk):
            slab = j * slabs_per_chunk + c
            gate = conv(slab)
            val = conv(val_slab0 + slab)
            act = gate * (1.0 / (1.0 + jnp.exp(-gate))) * val
            act_ref[:, slab * LANES:(slab + 1) * LANES] = act.astype(BF16)

    up_ref[:, 0:CARRY_ROWS, :] = up_ref[:, ts:ts + CARRY_ROWS, :]
    for rows in halves:
        y = h_ref[rows, :] + jnp.dot(act_ref[rows, :], w_down_ref[0], preferred_element_type=F32)
        if final_norm:
            y = _rmsnorm(y, fg_ref[...])
        o_ref[rows, :] = y


def _ffn(h, g, w_up, conv_w, conv_b, w_down, final_g, *, layer, seq, final_norm):
    rows = h.shape[0]
    ts = FFN_ROW_TILE
    return pl.pallas_call(
        partial(_ffn_kernel, tiles_per_seq=seq // ts, final_norm=final_norm),
        out_shape=jax.ShapeDtypeStruct(h.shape, h.dtype),
        grid=(rows // ts,),
        in_specs=[
            pl.BlockSpec((ts, D_MODEL), lambda i: (i, 0)),
            _layer_spec((1, D_MODEL), layer),
            _layer_spec((D_MODEL, 2 * D_FF), layer),
            _layer_spec((CONV_WIDTH, 2 * D_FF), layer),
            _layer_spec((1, 2 * D_FF), layer),
            _layer_spec((D_FF, D_MODEL), layer),
            pl.BlockSpec((1, D_MODEL), lambda i: (0, 0), pipeline_mode=pl.Buffered(1)),
        ],
        out_specs=pl.BlockSpec((ts, D_MODEL), lambda i: (i, 0)),
        scratch_shapes=[
            pltpu.VMEM((ts, D_MODEL), BF16),
            pltpu.VMEM((2 * D_FF // LANES, CARRY_ROWS + ts, LANES), F32),
            pltpu.VMEM((ts, D_FF), BF16),
        ],
        compiler_params=pltpu.CompilerParams(
            dimension_semantics=("arbitrary",), vmem_limit_bytes=VMEM_LIMIT_BYTES),
        name="conv_ffn",
    )(h, g, w_up, conv_w, conv_b, w_down, final_g)


def _outfold_kernel(pool_w_ref, pool_scale_ref, w_out_ref, o_ref):
    for gi in range(len(POOL_WINDOWS)):
        rows = slice(gi * POOL_GROUP_DIM, (gi + 1) * POOL_GROUP_DIM)
        a = pool_w_ref[0, gi] * pool_scale_ref[0, :, rows]
        o_ref[0, rows, :] = jnp.dot(a, w_out_ref[0, rows, :], preferred_element_type=F32,
                                    precision=lax.Precision.HIGHEST).astype(BF16)
    o_ref[0, POOL_WIDTH:, :] = w_out_ref[0, POOL_WIDTH:, :].astype(BF16)


def _outfold(pool_w, pool_scale, w_out):
    depth = w_out.shape[0]
    return pl.pallas_call(
        _outfold_kernel,
        out_shape=jax.ShapeDtypeStruct(w_out.shape, BF16),
        grid=(depth,),
        in_specs=[
            pl.BlockSpec((1,) + pool_w.shape[1:], lambda l: (l, 0, 0, 0)),
            pl.BlockSpec((1, 1, POOL_WIDTH), lambda l: (l, 0, 0)),
            pl.BlockSpec((1, D_MODEL, D_MODEL), lambda l: (l, 0, 0)),
        ],
        out_specs=pl.BlockSpec((1, D_MODEL, D_MODEL), lambda l: (l, 0, 0)),
        compiler_params=pltpu.CompilerParams(
            dimension_semantics=("arbitrary",), vmem_limit_bytes=VMEM_LIMIT_BYTES),
        name="outfold",
    )(pool_w, pool_scale, w_out)


def kernel(x, mem, norm_mix_g, w_in, pool_w, pool_scale, sgu_g, sgu_w, sgu_b, w_out, norm_xattn_g, mem_norm_g,
           wq, wk, wv, wo, norm_ffn_g, w_up, conv_w, conv_b, w_down, final_norm_g):
    batch, seq, d_model = x.shape
    depth = w_in.shape[0]
    assert d_model == D_MODEL and seq % (ROW_TILE * XATTN_SUBTILES) == 0 and seq % (ROW_TILE * MIXER_SUBTILES) == 0
    assert seq % FFN_ROW_TILE == 0 and ROW_TILE % CHUNK == 0 and D_FF % FF_CHUNK == 0

    def bf(w):
        return w.astype(BF16)

    def row(p):
        return p[:, None, :]

    wqk, vo = _memkv(mem, row(mem_norm_g), bf(wk), bf(wv), bf(wq), bf(wo))
    w_out_folded = _outfold(pool_w, row(pool_scale), w_out)
    sgu_b_full = jnp.repeat(jnp.swapaxes(sgu_b, 1, 2), SGU_HEAD_DIM, axis=2)
    w_in_b, sgu_w_b, w_up_b, w_down_b = bf(w_in), bf(sgu_w), bf(w_up), bf(w_down)

    h = x.reshape(batch * seq, d_model)
    for l in range(depth):
        h = _mixer(h, row(norm_mix_g), w_in_b, row(sgu_g), sgu_w_b, sgu_b_full, w_out_folded, layer=l, seq=seq)
        h = _xattn(h, row(norm_xattn_g), wqk, vo, layer=l, seq=seq)
        h = _ffn(h, row(norm_ffn_g), w_up_b, conv_w, row(conv_b), w_down_b, final_norm_g[None],
                 layer=l, seq=seq, final_norm=(l == depth - 1))
    return h.reshape(batch, seq, d_model)
```

```python
from functools import partial

import jax
import jax.numpy as jnp
from jax import lax
from jax.experimental import pallas as pl
from jax.experimental.pallas import tpu as pltpu

D_MODEL = 1024
N_MEM = 256
POOL_WIDTH = 512
POOL_WINDOWS = (2, 4, 8, 16)
POOL_GROUP_DIM = 128
POOL_LEVEL_BUFFERS = 6
POOL_HALO = 32
SGU_WIDTH = 512
SGU_HEADS = 4
SGU_HEAD_DIM = 128
CHUNK = 128
D_IN_PROJ = POOL_WIDTH + 2 * SGU_WIDTH
XATTN_HEADS = 4
XATTN_HEAD_DIM = 256
D_FF = 2816
CONV_WIDTH = 3
EPS = 1e-6

ROW_TILE = 512
MIXER_SUBTILES = 2
XATTN_SUBTILES = 4
FFN_ROW_TILE = 512
FF_CHUNK = 256
LANES = 128
CARRY_ROWS = 8
VMEM_LIMIT_BYTES = 56 * 1024 * 1024

BF16 = jnp.bfloat16
F32 = jnp.float32


def _rmsnorm(x, g):
    return x * lax.rsqrt(jnp.mean(x * x, axis=-1, keepdims=True) + EPS) * g


def _gelu_exact(x):
    return 0.5 * x * (1.0 + lax.erf(x * (0.5 ** 0.5)))


def _layer_spec(tail, layer):
    return pl.BlockSpec((1,) + tail, lambda i: (layer,) + (0,) * len(tail), pipeline_mode=pl.Buffered(1))


def _mixer_kernel(h_ref, g_ref, w_in_ref, sgu_g_ref, sgu_w_ref, sgu_b_ref, w_out_ref, o_ref,
                  xn_ref, u_ref, vn_ref, pbuf_ref, ptmp_ref, ycat_ref, *, tiles_per_seq):
    ts = h_ref.shape[0]
    sub = ts // MIXER_SUBTILES
    i = pl.program_id(0)
    tile_in_seq = i % tiles_per_seq
    u_cols = slice(POOL_WIDTH, POOL_WIDTH + SGU_WIDTH)
    v_cols = slice(POOL_WIDTH + SGU_WIDTH, D_IN_PROJ)
    tri = (lax.broadcasted_iota(jnp.int32, (CHUNK, CHUNK), 0)
           >= lax.broadcasted_iota(jnp.int32, (CHUNK, CHUNK), 1))

    @pl.when(tile_in_seq == 0)
    def _():
        pbuf_ref[:, 0:POOL_HALO, :] = jnp.zeros((len(POOL_WINDOWS), POOL_HALO, POOL_GROUP_DIM), F32)

    for r in range(MIXER_SUBTILES):
        rows = slice(r * sub, (r + 1) * sub)
        xn_ref[rows, :] = _rmsnorm(h_ref[rows, :], g_ref[0]).astype(BF16)

        v = _gelu_exact(jnp.dot(xn_ref[rows, :], w_in_ref[0, :, v_cols], preferred_element_type=F32))
        mu = jnp.mean(v, axis=-1, keepdims=True)
        vc = v - mu
        vn_ref[rows, :] = (vc * lax.rsqrt(jnp.mean(vc * vc, axis=-1, keepdims=True) + EPS)
                           * sgu_g_ref[0]).astype(BF16)
        p_all = jnp.dot(xn_ref[rows, :], w_in_ref[0, :, :POOL_WIDTH], preferred_element_type=F32)
        for gi in range(len(POOL_WINDOWS)):
            pbuf_ref[gi, POOL_HALO + r * sub:POOL_HALO + (r + 1) * sub, :] = (
                p_all[:, gi * POOL_GROUP_DIM:(gi + 1) * POOL_GROUP_DIM])
        u_ref[rows, :] = _gelu_exact(jnp.dot(xn_ref[rows, :], w_in_ref[0, :, u_cols], preferred_element_type=F32))

    for r in range(MIXER_SUBTILES):
        rows = slice(r * sub, (r + 1) * sub)

        base = r * sub
        n_tmp = 0
        for gi, win in enumerate(POOL_WINDOWS):
            lanes = slice(gi * POOL_GROUP_DIM, (gi + 1) * POOL_GROUP_DIM)

            def read(lo, n, gi=gi):
                return pbuf_ref[gi, base + lo:base + lo + n, :]

            span, first = 1, 0
            while 2 * span < win:
                first += 8
                n = POOL_HALO + sub - first
                tmp = ptmp_ref.at[r, n_tmp]
                n_tmp += 1
                tmp[first:first + n, :] = read(first, n) + read(first - span, n)

                def read(lo, n, tmp=tmp):
                    return tmp[lo:lo + n, :]

                span *= 2
            s = read(POOL_HALO, sub) + read(POOL_HALO - span, sub)
            p = pbuf_ref[gi, base + POOL_HALO:base + POOL_HALO + sub, :]
            if r == 0:
                nh = POOL_WINDOWS[-1]
                t1 = lax.broadcasted_iota(jnp.int32, (nh, 1), 0) + 1
                count = jnp.where(tile_in_seq == 0, jnp.minimum(t1, win), win)
                head = s[:nh, :] * (1.0 / count.astype(F32)) - p[:nh, :]
                ycat_ref[0:nh, lanes] = head.astype(BF16)
                ycat_ref[nh:sub, lanes] = (s[nh:, :] * (1.0 / win) - p[nh:, :]).astype(BF16)
            else:
                ycat_ref[rows, lanes] = (s * (1.0 / win) - p).astype(BF16)

        for hd in range(SGU_HEADS):
            lanes = slice(hd * SGU_HEAD_DIM, (hd + 1) * SGU_HEAD_DIM)
            w_masked = jnp.where(tri, sgu_w_ref[0, hd], jnp.zeros((), BF16))
            bias = sgu_b_ref[0, :, lanes]
            for c in range(sub // CHUNK):
                crows = slice(r * sub + c * CHUNK, r * sub + (c + 1) * CHUNK)
                z = jnp.dot(w_masked, vn_ref[crows, lanes], preferred_element_type=F32) + bias
                ycat_ref[crows, POOL_WIDTH + hd * SGU_HEAD_DIM:POOL_WIDTH + (hd + 1) * SGU_HEAD_DIM] = (
                    u_ref[crows, lanes] * z).astype(BF16)

        o_ref[rows, :] = h_ref[rows, :] + jnp.dot(ycat_ref[rows, :], w_out_ref[0], preferred_element_type=F32)

    pbuf_ref[:, 0:POOL_HALO, :] = pbuf_ref[:, ts:, :]


def _mixer(h, g, w_in, sgu_g, sgu_w, sgu_b_full, w_out, *, layer, seq):
    rows = h.shape[0]
    ts = ROW_TILE * MIXER_SUBTILES
    return pl.pallas_call(
        partial(_mixer_kernel, tiles_per_seq=seq // ts),
        out_shape=jax.ShapeDtypeStruct(h.shape, h.dtype),
        grid=(rows // ts,),
        in_specs=[
            pl.BlockSpec((ts, D_MODEL), lambda i: (i, 0)),
            _layer_spec((1, D_MODEL), layer),
            _layer_spec((D_MODEL, D_IN_PROJ), layer),
            _layer_spec((1, SGU_WIDTH), layer),
            _layer_spec((SGU_HEADS, CHUNK, CHUNK), layer),
            _layer_spec((CHUNK, SGU_WIDTH), layer),
            _layer_spec((D_MODEL, D_MODEL), layer),
        ],
        out_specs=pl.BlockSpec((ts, D_MODEL), lambda i: (i, 0)),
        scratch_shapes=[
            pltpu.VMEM((ts, D_MODEL), BF16),
            pltpu.VMEM((ts, SGU_WIDTH), F32),
            pltpu.VMEM((ts, SGU_WIDTH), BF16),
            pltpu.VMEM((len(POOL_WINDOWS), POOL_HALO + ts, POOL_GROUP_DIM), F32),
            pltpu.VMEM((MIXER_SUBTILES, POOL_LEVEL_BUFFERS, POOL_HALO + ts // MIXER_SUBTILES, POOL_GROUP_DIM), F32),
            pltpu.VMEM((ts, D_MODEL), BF16),
        ],
        compiler_params=pltpu.CompilerParams(
            dimension_semantics=("arbitrary",), vmem_limit_bytes=VMEM_LIMIT_BYTES),
        name="mixer",
    )(h, g, w_in, sgu_g, sgu_w, sgu_b_full, w_out)


def _memkv_kernel(mem_ref, g_ref, wk_ref, wv_ref, wq_ref, wo_ref, wqk_ref, vo_ref):
    memn = _rmsnorm(mem_ref[0], g_ref[0]).astype(BF16)
    k = jnp.dot(memn, wk_ref[0], preferred_element_type=F32).astype(BF16)
    v = jnp.dot(memn, wv_ref[0], preferred_element_type=F32).astype(BF16)
    for hd in range(XATTN_HEADS):
        cols = slice(hd * XATTN_HEAD_DIM, (hd + 1) * XATTN_HEAD_DIM)
        qk = lax.dot_general(wq_ref[0, :, cols], k[:, cols], (((1,), (1,)), ((), ())),
                             preferred_element_type=F32)
        wqk_ref[0, 0, :, hd * N_MEM:(hd + 1) * N_MEM] = (qk * (XATTN_HEAD_DIM ** -0.5)).astype(BF16)
        vo_ref[0, 0, hd * N_MEM:(hd + 1) * N_MEM, :] = jnp.dot(
            v[:, cols], wo_ref[0, cols, :], preferred_element_type=F32).astype(BF16)


def _memkv(mem, mem_g, wk, wv, wq, wo):
    depth = wk.shape[0]
    batch = mem.shape[0]
    w_spec = pl.BlockSpec((1, D_MODEL, D_MODEL), lambda l, b: (l, 0, 0))
    return pl.pallas_call(
        _memkv_kernel,
        out_shape=(jax.ShapeDtypeStruct((depth, batch, D_MODEL, XATTN_HEADS * N_MEM), BF16),
                   jax.ShapeDtypeStruct((depth, batch, XATTN_HEADS * N_MEM, D_MODEL), BF16)),
        grid=(depth, batch),
        in_specs=[
            pl.BlockSpec((1, N_MEM, D_MODEL), lambda l, b: (b, 0, 0)),
            pl.BlockSpec((1, 1, D_MODEL), lambda l, b: (l, 0, 0)),
            w_spec, w_spec, w_spec, w_spec,
        ],
        out_specs=(pl.BlockSpec((1, 1, D_MODEL, XATTN_HEADS * N_MEM), lambda l, b: (l, b, 0, 0)),
                   pl.BlockSpec((1, 1, XATTN_HEADS * N_MEM, D_MODEL), lambda l, b: (l, b, 0, 0))),
        compiler_params=pltpu.CompilerParams(
            dimension_semantics=("arbitrary", "arbitrary"), vmem_limit_bytes=VMEM_LIMIT_BYTES),
        name="memkv",
    )(mem, mem_g, wk, wv, wq, wo)


def _xattn_kernel(h_ref, g_ref, wqk_ref, vo_ref, o_ref, pr_ref):
    sub = h_ref.shape[0] // XATTN_SUBTILES

    def scores(r):
        rows = slice(r * sub, (r + 1) * sub)
        xn = _rmsnorm(h_ref[rows, :], g_ref[0]).astype(BF16)
        s = jnp.dot(xn, wqk_ref[0, 0], preferred_element_type=F32)
        for hd in range(XATTN_HEADS):
            cols = slice(hd * N_MEM, (hd + 1) * N_MEM)
            sh = s[:, cols]
            e = jnp.exp(sh - jnp.max(sh, axis=-1, keepdims=True))
            pr_ref[rows, cols] = (e / jnp.sum(e, axis=-1, keepdims=True)).astype(BF16)

    def attend(r):
        rows = slice(r * sub, (r + 1) * sub)
        o_ref[rows, :] = h_ref[rows, :] + jnp.dot(pr_ref[rows, :], vo_ref[0, 0], preferred_element_type=F32)

    scores(0)
    for r in range(XATTN_SUBTILES):
        if r + 1 < XATTN_SUBTILES:
            scores(r + 1)
        attend(r)


def _xattn(h, g, wqk, vo, *, layer, seq):
    rows = h.shape[0]
    ts = ROW_TILE * XATTN_SUBTILES
    tiles_per_seq = seq // ts
    kv_spec = pl.BlockSpec((1, 1) + wqk.shape[2:], lambda i: (layer, i // tiles_per_seq, 0, 0))
    return pl.pallas_call(
        _xattn_kernel,
        out_shape=jax.ShapeDtypeStruct(h.shape, h.dtype),
        grid=(rows // ts,),
        in_specs=[
            pl.BlockSpec((ts, D_MODEL), lambda i: (i, 0)),
            _layer_spec((1, D_MODEL), layer),
            kv_spec,
            kv_spec,
        ],
        out_specs=pl.BlockSpec((ts, D_MODEL), lambda i: (i, 0)),
        scratch_shapes=[pltpu.VMEM((ts, XATTN_HEADS * N_MEM), BF16)],
        compiler_params=pltpu.CompilerParams(
            dimension_semantics=("arbitrary",), vmem_limit_bytes=VMEM_LIMIT_BYTES),
        name="xattn",
    )(h, g, wqk, vo)


def _ffn_kernel(h_ref, g_ref, w_up_ref, conv_w_ref, conv_b_ref, w_down_ref, fg_ref, o_ref,
                xn_ref, up_ref, act_ref, *, tiles_per_seq, final_norm):
    ts = h_ref.shape[0]
    i = pl.program_id(0)
    slabs_per_chunk = FF_CHUNK // LANES
    val_slab0 = D_FF // LANES
    halves = (slice(0, ts // 2), slice(ts // 2, ts))

    @pl.when(i % tiles_per_seq == 0)
    def _():
        up_ref[:, 0:CARRY_ROWS, :] = jnp.zeros((up_ref.shape[0], CARRY_ROWS, LANES), F32)

    def up_project(first_slab, rows=slice(0, FFN_ROW_TILE)):
        c0 = first_slab * LANES
        up = jnp.dot(xn_ref[rows, :], w_up_ref[0, :, c0:c0 + FF_CHUNK], preferred_element_type=F32)
        for c in range(slabs_per_chunk):
            up_ref[first_slab + c, CARRY_ROWS + rows.start:CARRY_ROWS + rows.stop, :] = (
                up[:, c * LANES:(c + 1) * LANES])

    def conv(slab):
        cols = slice(slab * LANES, (slab + 1) * LANES)
        cur = up_ref[slab, CARRY_ROWS:CARRY_ROWS + ts, :]
        prev1 = up_ref[slab, CARRY_ROWS - 1:CARRY_ROWS - 1 + ts, :]
        prev2 = up_ref[slab, CARRY_ROWS - 2:CARRY_ROWS - 2 + ts, :]
        return (conv_b_ref[0, :, cols] + conv_w_ref[0, 0:1, cols] * prev2 + conv_w_ref[0, 1:2, cols] * prev1
                + conv_w_ref[0, 2:3, cols] * cur)

    for rows in halves:
        xn_ref[rows, :] = _rmsnorm(h_ref[rows, :], g_ref[0]).astype(BF16)
        up_project(0, rows)
        up_project(val_slab0, rows)
    for j in range(D_FF // FF_CHUNK):
        if j > 0:
            up_project(j * slabs_per_chunk)
            up_project(val_slab0 + j * slabs_per_chunk)
        for c in range(slabs_per_chunk):
            slab = j * slabs_per_chunk + c
            gate = conv(slab)
            val = conv(val_slab0 + slab)
            act = gate * (1.0 / (1.0 + jnp.exp(-gate))) * val
            act_ref[:, slab * LANES:(slab + 1) * LANES] = act.astype(BF16)

    up_ref[:, 0:CARRY_ROWS, :] = up_ref[:, ts:ts + CARRY_ROWS, :]
    for rows in halves:
        y = h_ref[rows, :] + jnp.dot(act_ref[rows, :], w_down_ref[0], preferred_element_type=F32)
        if final_norm:
            y = _rmsnorm(y, fg_ref[...])
        o_ref[rows, :] = y


def _ffn(h, g, w_up, conv_w, conv_b, w_down, final_g, *, layer, seq, final_norm):
    rows = h.shape[0]
    ts = FFN_ROW_TILE
    return pl.pallas_call(
        partial(_ffn_kernel, tiles_per_seq=seq // ts, final_norm=final_norm),
        out_shape=jax.ShapeDtypeStruct(h.shape, h.dtype),
        grid=(rows // ts,),
        in_specs=[
            pl.BlockSpec((ts, D_MODEL), lambda i: (i, 0)),
            _layer_spec((1, D_MODEL), layer),
            _layer_spec((D_MODEL, 2 * D_FF), layer),
            _layer_spec((CONV_WIDTH, 2 * D_FF), layer),
            _layer_spec((1, 2 * D_FF), layer),
            _layer_spec((D_FF, D_MODEL), layer),
            pl.BlockSpec((1, D_MODEL), lambda i: (0, 0), pipeline_mode=pl.Buffered(1)),
        ],
        out_specs=pl.BlockSpec((ts, D_MODEL), lambda i: (i, 0)),
        scratch_shapes=[
            pltpu.VMEM((ts, D_MODEL), BF16),
            pltpu.VMEM((2 * D_FF // LANES, CARRY_ROWS + ts, LANES), F32),
            pltpu.VMEM((ts, D_FF), BF16),
        ],
        compiler_params=pltpu.CompilerParams(
            dimension_semantics=("arbitrary",), vmem_limit_bytes=VMEM_LIMIT_BYTES),
        name="conv_ffn",
    )(h, g, w_up, conv_w, conv_b, w_down, final_g)


def _outfold_kernel(pool_w_ref, pool_scale_ref, w_out_ref, o_ref):
    for gi in range(len(POOL_WINDOWS)):
        rows = slice(gi * POOL_GROUP_DIM, (gi + 1) * POOL_GROUP_DIM)
        a = pool_w_ref[0, gi] * pool_scale_ref[0, :, rows]
        o_ref[0, rows, :] = jnp.dot(a, w_out_ref[0, rows, :], preferred_element_type=F32,
                                    precision=lax.Precision.HIGHEST).astype(BF16)
    o_ref[0, POOL_WIDTH:, :] = w_out_ref[0, POOL_WIDTH:, :].astype(BF16)


def _outfold(pool_w, pool_scale, w_out):
    depth = w_out.shape[0]
    return pl.pallas_call(
        _outfold_kernel,
        out_shape=jax.ShapeDtypeStruct(w_out.shape, BF16),
        grid=(depth,),
        in_specs=[
            pl.BlockSpec((1,) + pool_w.shape[1:], lambda l: (l, 0, 0, 0)),
            pl.BlockSpec((1, 1, POOL_WIDTH), lambda l: (l, 0, 0)),
            pl.BlockSpec((1, D_MODEL, D_MODEL), lambda l: (l, 0, 0)),
        ],
        out_specs=pl.BlockSpec((1, D_MODEL, D_MODEL), lambda l: (l, 0, 0)),
        compiler_params=pltpu.CompilerParams(
            dimension_semantics=("arbitrary",), vmem_limit_bytes=VMEM_LIMIT_BYTES),
        name="outfold",
    )(pool_w, pool_scale, w_out)


def kernel(x, mem, norm_mix_g, w_in, pool_w, pool_scale, sgu_g, sgu_w, sgu_b, w_out, norm_xattn_g, mem_norm_g,
           wq, wk, wv, wo, norm_ffn_g, w_up, conv_w, conv_b, w_down, final_norm_g):
    batch, seq, d_model = x.shape
    depth = w_in.shape[0]
    assert d_model == D_MODEL and seq % (ROW_TILE * XATTN_SUBTILES) == 0 and seq % (ROW_TILE * MIXER_SUBTILES) == 0
    assert seq % FFN_ROW_TILE == 0 and ROW_TILE % CHUNK == 0 and D_FF % FF_CHUNK == 0

    def bf(w):
        return w.astype(BF16)

    def row(p):
        return p[:, None, :]

    wqk, vo = _memkv(mem, row(mem_norm_g), bf(wk), bf(wv), bf(wq), bf(wo))
    w_out_folded = _outfold(pool_w, row(pool_scale), w_out)
    sgu_b_full = jnp.repeat(jnp.swapaxes(sgu_b, 1, 2), SGU_HEAD_DIM, axis=2)
    w_in_b, sgu_w_b, w_up_b, w_down_b = bf(w_in), bf(sgu_w), bf(w_up), bf(w_down)

    h = x.reshape(batch * seq, d_model)
    for l in range(depth):
        h = _mixer(h, row(norm_mix_g), w_in_b, row(sgu_g), sgu_w_b, sgu_b_full, w_out_folded, layer=l, seq=seq)
        h = _xattn(h, row(norm_xattn_g), wqk, vo, layer=l, seq=seq)
        h = _ffn(h, row(norm_ffn_g), w_up_b, conv_w, row(conv_b), w_down_b, final_norm_g[None],
                 layer=l, seq=seq, final_norm=(l == depth - 1))
    return h.reshape(batch, seq, d_model)
```

```python
from functools import partial

import jax
import jax.numpy as jnp
from jax import lax
from jax.experimental import pallas as pl
from jax.experimental.pallas import tpu as pltpu

D_MODEL = 1024
N_MEM = 256
POOL_WIDTH = 512
POOL_WINDOWS = (2, 4, 8, 16)
POOL_GROUP_DIM = 128
POOL_LEVEL_BUFFERS = 6
POOL_HALO = 32
SGU_WIDTH = 512
SGU_HEADS = 4
SGU_HEAD_DIM = 128
CHUNK = 128
D_IN_PROJ = POOL_WIDTH + 2 * SGU_WIDTH
XATTN_HEADS = 4
XATTN_HEAD_DIM = 256
D_FF = 2816
CONV_WIDTH = 3
EPS = 1e-6

ROW_TILE = 512
MIXER_SUBTILES = 2
XATTN_SUBTILES = 4
FFN_ROW_TILE = 512
FF_CHUNK = 256
LANES = 128
CARRY_ROWS = 8
VMEM_LIMIT_BYTES = 56 * 1024 * 1024

BF16 = jnp.bfloat16
F32 = jnp.float32


def _rmsnorm(x, g):
    return x * lax.rsqrt(jnp.mean(x * x, axis=-1, keepdims=True) + EPS) * g


def _gelu_exact(x):
    return 0.5 * x * (1.0 + lax.erf(x * (0.5 ** 0.5)))


def _layer_spec(tail, layer):
    return pl.BlockSpec((1,) + tail, lambda i: (layer,) + (0,) * len(tail), pipeline_mode=pl.Buffered(1))


def _mixer_kernel(h_ref, g_ref, w_in_ref, sgu_g_ref, sgu_w_ref, sgu_b_ref, w_out_ref, o_ref,
                  xn_ref, u_ref, vn_ref, pbuf_ref, ptmp_ref, ycat_ref, *, tiles_per_seq):
    ts = h_ref.shape[0]
    sub = ts // MIXER_SUBTILES
    i = pl.program_id(0)
    tile_in_seq = i % tiles_per_seq
    u_cols = slice(POOL_WIDTH, POOL_WIDTH + SGU_WIDTH)
    v_cols = slice(POOL_WIDTH + SGU_WIDTH, D_IN_PROJ)
    tri = (lax.broadcasted_iota(jnp.int32, (CHUNK, CHUNK), 0)
           >= lax.broadcasted_iota(jnp.int32, (CHUNK, CHUNK), 1))

    @pl.when(tile_in_seq == 0)
    def _():
        pbuf_ref[:, 0:POOL_HALO, :] = jnp.zeros((len(POOL_WINDOWS), POOL_HALO, POOL_GROUP_DIM), F32)

    for r in range(MIXER_SUBTILES):
        rows = slice(r * sub, (r + 1) * sub)
        xn_ref[rows, :] = _rmsnorm(h_ref[rows, :], g_ref[0]).astype(BF16)

        v = _gelu_exact(jnp.dot(xn_ref[rows, :], w_in_ref[0, :, v_cols], preferred_element_type=F32))
        mu = jnp.mean(v, axis=-1, keepdims=True)
        vc = v - mu
        vn_ref[rows, :] = (vc * lax.rsqrt(jnp.mean(vc * vc, axis=-1, keepdims=True) + EPS)
                           * sgu_g_ref[0]).astype(BF16)
        p_all = jnp.dot(xn_ref[rows, :], w_in_ref[0, :, :POOL_WIDTH], preferred_element_type=F32)
        for gi in range(len(POOL_WINDOWS)):
            pbuf_ref[gi, POOL_HALO + r * sub:POOL_HALO + (r + 1) * sub, :] = (
                p_all[:, gi * POOL_GROUP_DIM:(gi + 1) * POOL_GROUP_DIM])
        u_ref[rows, :] = _gelu_exact(jnp.dot(xn_ref[rows, :], w_in_ref[0, :, u_cols], preferred_element_type=F32))

        base = r * sub
        n_tmp = 0
        for gi, win in enumerate(POOL_WINDOWS):
            lanes = slice(gi * POOL_GROUP_DIM, (gi + 1) * POOL_GROUP_DIM)

            def read(lo, n, gi=gi):
                return pbuf_ref[gi, base + lo:base + lo + n, :]

            span, first = 1, 0
            while 2 * span < win:
                first += 8
                n = POOL_HALO + sub - first
                tmp = ptmp_ref.at[r, n_tmp]
                n_tmp += 1
                tmp[first:first + n, :] = read(first, n) + read(first - span, n)

                def read(lo, n, tmp=tmp):
                    return tmp[lo:lo + n, :]

                span *= 2
            s = read(POOL_HALO, sub) + read(POOL_HALO - span, sub)
            p = pbuf_ref[gi, base + POOL_HALO:base + POOL_HALO + sub, :]
            if r == 0:
                nh = POOL_WINDOWS[-1]
                t1 = lax.broadcasted_iota(jnp.int32, (nh, 1), 0) + 1
                count = jnp.where(tile_in_seq == 0, jnp.minimum(t1, win), win)
                head = s[:nh, :] * (1.0 / count.astype(F32)) - p[:nh, :]
                ycat_ref[0:nh, lanes] = head.astype(BF16)
                ycat_ref[nh:sub, lanes] = (s[nh:, :] * (1.0 / win) - p[nh:, :]).astype(BF16)
            else:
                ycat_ref[rows, lanes] = (s * (1.0 / win) - p).astype(BF16)

        for hd in range(SGU_HEADS):
            lanes = slice(hd * SGU_HEAD_DIM, (hd + 1) * SGU_HEAD_DIM)
            w_masked = jnp.where(tri, sgu_w_ref[0, hd], jnp.zeros((), BF16))
            bias = sgu_b_ref[0, :, lanes]
            for c in range(sub // CHUNK):
                crows = slice(r * sub + c * CHUNK, r * sub + (c + 1) * CHUNK)
                z = jnp.dot(w_masked, vn_ref[crows, lanes], preferred_element_type=F32) + bias
                ycat_ref[crows, POOL_WIDTH + hd * SGU_HEAD_DIM:POOL_WIDTH + (hd + 1) * SGU_HEAD_DIM] = (
                    u_ref[crows, lanes] * z).astype(BF16)

        o_ref[rows, :] = h_ref[rows, :] + jnp.dot(ycat_ref[rows, :], w_out_ref[0], preferred_element_type=F32)

    pbuf_ref[:, 0:POOL_HALO, :] = pbuf_ref[:, ts:, :]


def _mixer(h, g, w_in, sgu_g, sgu_w, sgu_b_full, w_out, *, layer, seq):
    rows = h.shape[0]
    ts = ROW_TILE * MIXER_SUBTILES
    return pl.pallas_call(
        partial(_mixer_kernel, tiles_per_seq=seq // ts),
        out_shape=jax.ShapeDtypeStruct(h.shape, h.dtype),
        grid=(rows // ts,),
        in_specs=[
            pl.BlockSpec((ts, D_MODEL), lambda i: (i, 0)),
            _layer_spec((1, D_MODEL), layer),
            _layer_spec((D_MODEL, D_IN_PROJ), layer),
            _layer_spec((1, SGU_WIDTH), layer),
            _layer_spec((SGU_HEADS, CHUNK, CHUNK), layer),
            _layer_spec((CHUNK, SGU_WIDTH), layer),
            _layer_spec((D_MODEL, D_MODEL), layer),
        ],
        out_specs=pl.BlockSpec((ts, D_MODEL), lambda i: (i, 0)),
        scratch_shapes=[
            pltpu.VMEM((ts, D_MODEL), BF16),
            pltpu.VMEM((ts, SGU_WIDTH), F32),
            pltpu.VMEM((ts, SGU_WIDTH), BF16),
            pltpu.VMEM((len(POOL_WINDOWS), POOL_HALO + ts, POOL_GROUP_DIM), F32),
            pltpu.VMEM((MIXER_SUBTILES, POOL_LEVEL_BUFFERS, POOL_HALO + ts // MIXER_SUBTILES, POOL_GROUP_DIM), F32),
            pltpu.VMEM((ts, D_MODEL), BF16),
        ],
        compiler_params=pltpu.CompilerParams(
            dimension_semantics=("arbitrary",), vmem_limit_bytes=VMEM_LIMIT_BYTES),
        name="mixer",
    )(h, g, w_in, sgu_g, sgu_w, sgu_b_full, w_out)


def _memkv_kernel(mem_ref, g_ref, wk_ref, wv_ref, wq_ref, wo_ref, wqk_ref, vo_ref):
    memn = _rmsnorm(mem_ref[0], g_ref[0]).astype(BF16)
    k = jnp.dot(memn, wk_ref[0], preferred_element_type=F32).astype(BF16)
    v = jnp.dot(memn, wv_ref[0], preferred_element_type=F32).astype(BF16)
    for hd in range(XATTN_HEADS):
        cols = slice(hd * XATTN_HEAD_DIM, (hd + 1) * XATTN_HEAD_DIM)
        qk = lax.dot_general(wq_ref[0, :, cols], k[:, cols], (((1,), (1,)), ((), ())),
                             preferred_element_type=F32)
        wqk_ref[0, 0, :, hd * N_MEM:(hd + 1) * N_MEM] = (qk * (XATTN_HEAD_DIM ** -0.5)).astype(BF16)
        vo_ref[0, 0, hd * N_MEM:(hd + 1) * N_MEM, :] = jnp.dot(
            v[:, cols], wo_ref[0, cols, :], preferred_element_type=F32).astype(BF16)


def _memkv(mem, mem_g, wk, wv, wq, wo):
    depth = wk.shape[0]
    batch = mem.shape[0]
    w_spec = pl.BlockSpec((1, D_MODEL, D_MODEL), lambda l, b: (l, 0, 0))
    return pl.pallas_call(
        _memkv_kernel,
        out_shape=(jax.ShapeDtypeStruct((depth, batch, D_MODEL, XATTN_HEADS * N_MEM), BF16),
                   jax.ShapeDtypeStruct((depth, batch, XATTN_HEADS * N_MEM, D_MODEL), BF16)),
        grid=(depth, batch),
        in_specs=[
            pl.BlockSpec((1, N_MEM, D_MODEL), lambda l, b: (b, 0, 0)),
            pl.BlockSpec((1, 1, D_MODEL), lambda l, b: (l, 0, 0)),
            w_spec, w_spec, w_spec, w_spec,
        ],
        out_specs=(pl.BlockSpec((1, 1, D_MODEL, XATTN_HEADS * N_MEM), lambda l, b: (l, b, 0, 0)),
                   pl.BlockSpec((1, 1, XATTN_HEADS * N_MEM, D_MODEL), lambda l, b: (l, b, 0, 0))),
        compiler_params=pltpu.CompilerParams(
            dimension_semantics=("arbitrary", "arbitrary"), vmem_limit_bytes=VMEM_LIMIT_BYTES),
        name="memkv",
    )(mem, mem_g, wk, wv, wq, wo)


def _xattn_kernel(h_ref, g_ref, wqk_ref, vo_ref, o_ref, pr_ref):
    sub = h_ref.shape[0] // XATTN_SUBTILES

    def scores(r):
        rows = slice(r * sub, (r + 1) * sub)
        xn = _rmsnorm(h_ref[rows, :], g_ref[0]).astype(BF16)
        s = jnp.dot(xn, wqk_ref[0, 0], preferred_element_type=F32)
        for hd in range(XATTN_HEADS):
            cols = slice(hd * N_MEM, (hd + 1) * N_MEM)
            sh = s[:, cols]
            e = jnp.exp(sh - jnp.max(sh, axis=-1, keepdims=True))
            pr_ref[rows, cols] = (e / jnp.sum(e, axis=-1, keepdims=True)).astype(BF16)

    def attend(r):
        rows = slice(r * sub, (r + 1) * sub)
        o_ref[rows, :] = h_ref[rows, :] + jnp.dot(pr_ref[rows, :], vo_ref[0, 0], preferred_element_type=F32)

    scores(0)
    for r in range(XATTN_SUBTILES):
        if r + 1 < XATTN_SUBTILES:
            scores(r + 1)
        attend(r)


def _xattn(h, g, wqk, vo, *, layer, seq):
    rows = h.shape[0]
    ts = ROW_TILE * XATTN_SUBTILES
    tiles_per_seq = seq // ts
    kv_spec = pl.BlockSpec((1, 1) + wqk.shape[2:], lambda i: (layer, i // tiles_per_seq, 0, 0))
    return pl.pallas_call(
        _xattn_kernel,
        out_shape=jax.ShapeDtypeStruct(h.shape, h.dtype),
        grid=(rows // ts,),
        in_specs=[
            pl.BlockSpec((ts, D_MODEL), lambda i: (i, 0)),
            _layer_spec((1, D_MODEL), layer),
            kv_spec,
            kv_spec,
        ],
        out_specs=pl.BlockSpec((ts, D_MODEL), lambda i: (i, 0)),
        scratch_shapes=[pltpu.VMEM((ts, XATTN_HEADS * N_MEM), BF16)],
        compiler_params=pltpu.CompilerParams(
            dimension_semantics=("arbitrary",), vmem_limit_bytes=VMEM_LIMIT_BYTES),
        name="xattn",
    )(h, g, wqk, vo)


def _ffn_kernel(h_ref, g_ref, w_up_ref, conv_w_ref, conv_b_ref, w_down_ref, fg_ref, o_ref,
                xn_ref, up_ref, act_ref, *, tiles_per_seq, final_norm):
    ts = h_ref.shape[0]
    i = pl.program_id(0)
    slabs_per_chunk = FF_CHUNK // LANES
    val_slab0 = D_FF // LANES
    halves = (slice(0, ts // 2), slice(ts // 2, ts))

    @pl.when(i % tiles_per_seq == 0)
    def _():
        up_ref[:, 0:CARRY_ROWS, :] = jnp.zeros((up_ref.shape[0], CARRY_ROWS, LANES), F32)

    def up_project(first_slab, rows=slice(0, FFN_ROW_TILE)):
        c0 = first_slab * LANES
        up = jnp.dot(xn_ref[rows, :], w_up_ref[0, :, c0:c0 + FF_CHUNK], preferred_element_type=F32)
        for c in range(slabs_per_chunk):
            up_ref[first_slab + c, CARRY_ROWS + rows.start:CARRY_ROWS + rows.stop, :] = (
                up[:, c * LANES:(c + 1) * LANES])

    def conv(slab):
        cols = slice(slab * LANES, (slab + 1) * LANES)
        cur = up_ref[slab, CARRY_ROWS:CARRY_ROWS + ts, :]
        prev1 = up_ref[slab, CARRY_ROWS - 1:CARRY_ROWS - 1 + ts, :]
        prev2 = up_ref[slab, CARRY_ROWS - 2:CARRY_ROWS - 2 + ts, :]
        return (conv_b_ref[0, :, cols] + conv_w_ref[0, 0:1, cols] * prev2 + conv_w_ref[0, 1:2, cols] * prev1
                + conv_w_ref[0, 2:3, cols] * cur)

    for rows in halves:
        xn_ref[rows, :] = _rmsnorm(h_ref[rows, :], g_ref[0]).astype(BF16)
        up_project(0, rows)
        up_project(val_slab0, rows)
    for j in range(D_FF // FF_CHUNK):
        if j > 0:
            up_project(j * slabs_per_chunk)
            up_project(val_slab0 + j * slabs_per_chunk)
        for c in range(slabs_per_chunk):
            slab = j * slabs_per_chunk + c
            gate = conv(slab)
            val = conv(val_slab0 + slab)
            act = gate * (1.0 / (1.0 + jnp.exp(-gate))) * val
            act_ref[:, slab * LANES:(slab + 1) * LANES] = act.astype(BF16)

    up_ref[:, 0:CARRY_ROWS, :] = up_ref[:, ts:ts + CARRY_ROWS, :]
    for rows in halves:
        y = h_ref[rows, :] + jnp.dot(act_ref[rows, :], w_down_ref[0], preferred_element_type=F32)
        if final_norm:
            y = _rmsnorm(y, fg_ref[...])
        o_ref[rows, :] = y


def _ffn(h, g, w_up, conv_w, conv_b, w_down, final_g, *, layer, seq, final_norm):
    rows = h.shape[0]
    ts = FFN_ROW_TILE
    return pl.pallas_call(
        partial(_ffn_kernel, tiles_per_seq=seq // ts, final_norm=final_norm),
        out_shape=jax.ShapeDtypeStruct(h.shape, h.dtype),
        grid=(rows // ts,),
        in_specs=[
            pl.BlockSpec((ts, D_MODEL), lambda i: (i, 0)),
            _layer_spec((1, D_MODEL), layer),
            _layer_spec((D_MODEL, 2 * D_FF), layer),
            _layer_spec((CONV_WIDTH, 2 * D_FF), layer),
            _layer_spec((1, 2 * D_FF), layer),
            _layer_spec((D_FF, D_MODEL), layer),
            pl.BlockSpec((1, D_MODEL), lambda i: (0, 0), pipeline_mode=pl.Buffered(1)),
        ],
        out_specs=pl.BlockSpec((ts, D_MODEL), lambda i: (i, 0)),
        scratch_shapes=[
            pltpu.VMEM((ts, D_MODEL), BF16),
            pltpu.VMEM((2 * D_FF // LANES, CARRY_ROWS + ts, LANES), F32),
            pltpu.VMEM((ts, D_FF), BF16),
        ],
        compiler_params=pltpu.CompilerParams(
            dimension_semantics=("arbitrary",), vmem_limit_bytes=VMEM_LIMIT_BYTES),
        name="conv_ffn",
    )(h, g, w_up, conv_w, conv_b, w_down, final_g)


def _outfold_kernel(pool_w_ref, pool_scale_ref, w_out_ref, o_ref):
    for gi in range(len(POOL_WINDOWS)):
        rows = slice(gi * POOL_GROUP_DIM, (gi + 1) * POOL_GROUP_DIM)
        a = pool_w_ref[0, gi] * pool_scale_ref[0, :, rows]
        o_ref[0, rows, :] = jnp.dot(a, w_out_ref[0, rows, :], preferred_element_type=F32,
                                    precision=lax.Precision.HIGHEST).astype(BF16)
    o_ref[0, POOL_WIDTH:, :] = w_out_ref[0, POOL_WIDTH:, :].astype(BF16)


def _outfold(pool_w, pool_scale, w_out):
    depth = w_out.shape[0]
    return pl.pallas_call(
        _outfold_kernel,
        out_shape=jax.ShapeDtypeStruct(w_out.shape, BF16),
        grid=(depth,),
        in_specs=[
            pl.BlockSpec((1,) + pool_w.shape[1:], lambda l: (l, 0, 0, 0)),
            pl.BlockSpec((1, 1, POOL_WIDTH), lambda l: (l, 0, 0)),
            pl.BlockSpec((1, D_MODEL, D_MODEL), lambda l: (l, 0, 0)),
        ],
        out_specs=pl.BlockSpec((1, D_MODEL, D_MODEL), lambda l: (l, 0, 0)),
        compiler_params=pltpu.CompilerParams(
            dimension_semantics=("arbitrary",), vmem_limit_bytes=VMEM_LIMIT_BYTES),
        name="outfold",
    )(pool_w, pool_scale, w_out)


def kernel(x, mem, norm_mix_g, w_in, pool_w, pool_scale, sgu_g, sgu_w, sgu_b, w_out, norm_xattn_g, mem_norm_g,
           wq, wk, wv, wo, norm_ffn_g, w_up, conv_w, conv_b, w_down, final_norm_g):
    batch, seq, d_model = x.shape
    depth = w_in.shape[0]
    assert d_model == D_MODEL and seq % (ROW_TILE * XATTN_SUBTILES) == 0 and seq % (ROW_TILE * MIXER_SUBTILES) == 0
    assert seq % FFN_ROW_TILE == 0 and ROW_TILE % CHUNK == 0 and D_FF % FF_CHUNK == 0

    def bf(w):
        return w.astype(BF16)

    def row(p):
        return p[:, None, :]

    wqk, vo = _memkv(mem, row(mem_norm_g), bf(wk), bf(wv), bf(wq), bf(wo))
    w_out_folded = _outfold(pool_w, row(pool_scale), w_out)
    sgu_b_full = jnp.repeat(jnp.swapaxes(sgu_b, 1, 2), SGU_HEAD_DIM, axis=2)
    w_in_b, sgu_w_b, w_up_b, w_down_b = bf(w_in), bf(sgu_w), bf(w_up), bf(w_down)

    h = x.reshape(batch * seq, d_model)
    for l in range(depth):
        h = _mixer(h, row(norm_mix_g), w_in_b, row(sgu_g), sgu_w_b, sgu_b_full, w_out_folded, layer=l, seq=seq)
        h = _xattn(h, row(norm_xattn_g), wqk, vo, layer=l, seq=seq)
        h = _ffn(h, row(norm_ffn_g), w_up_b, conv_w, row(conv_b), w_down_b, final_norm_g[None],
                 layer=l, seq=seq, final_norm=(l == depth - 1))
    return h.reshape(batch, seq, d_model)
```

```python
from functools import partial

import jax
import jax.numpy as jnp
from jax import lax
from jax.experimental import pallas as pl
from jax.experimental.pallas import tpu as pltpu

D_MODEL = 1024
N_MEM = 256
POOL_WIDTH = 512
POOL_WINDOWS = (2, 4, 8, 16)
POOL_GROUP_DIM = 128
POOL_LEVEL_BUFFERS = 6
POOL_HALO = 32
SGU_WIDTH = 512
SGU_HEADS = 4
SGU_HEAD_DIM = 128
CHUNK = 128
D_IN_PROJ = POOL_WIDTH + 2 * SGU_WIDTH
XATTN_HEADS = 4
XATTN_HEAD_DIM = 256
D_FF = 2816
CONV_WIDTH = 3
EPS = 1e-6

ROW_TILE = 512
MIXER_SUBTILES = 2
XATTN_SUBTILES = 4
FFN_ROW_TILE = 512
FF_CHUNK = 256
LANES = 128
CARRY_ROWS = 8
VMEM_LIMIT_BYTES = 56 * 1024 * 1024

BF16 = jnp.bfloat16
F32 = jnp.float32


def _rmsnorm(x, g):
    return x * lax.rsqrt(jnp.mean(x * x, axis=-1, keepdims=True) + EPS) * g


def _gelu_exact(x):
    return 0.5 * x * (1.0 + lax.erf(x * (0.5 ** 0.5)))


def _layer_spec(tail, layer):
    return pl.BlockSpec((1,) + tail, lambda i: (layer,) + (0,) * len(tail), pipeline_mode=pl.Buffered(1))


def _mixer_kernel(h_ref, g_ref, w_in_ref, sgu_g_ref, sgu_w_ref, sgu_b_ref, w_out_ref, o_ref,
                  pbuf_ref, ptmp_ref, ycat_ref, *, tiles_per_seq):
    ts = h_ref.shape[0]
    sub = ts // MIXER_SUBTILES
    i = pl.program_id(0)
    tile_in_seq = i % tiles_per_seq
    u_cols = slice(POOL_WIDTH, POOL_WIDTH + SGU_WIDTH)
    v_cols = slice(POOL_WIDTH + SGU_WIDTH, D_IN_PROJ)
    tri = (lax.broadcasted_iota(jnp.int32, (CHUNK, CHUNK), 0)
           >= lax.broadcasted_iota(jnp.int32, (CHUNK, CHUNK), 1))

    @pl.when(tile_in_seq == 0)
    def _():
        pbuf_ref[:, 0:POOL_HALO, :] = jnp.zeros((len(POOL_WINDOWS), POOL_HALO, POOL_GROUP_DIM), F32)

    for r in range(MIXER_SUBTILES):
        rows = slice(r * sub, (r + 1) * sub)
        xn = _rmsnorm(h_ref[rows, :], g_ref[0]).astype(BF16)

        v = _gelu_exact(jnp.dot(xn, w_in_ref[0, :, v_cols], preferred_element_type=F32))
        mu = jnp.mean(v, axis=-1, keepdims=True)
        vc = v - mu
        vn = (vc * lax.rsqrt(jnp.mean(vc * vc, axis=-1, keepdims=True) + EPS) * sgu_g_ref[0]).astype(BF16)
        p_all = jnp.dot(xn, w_in_ref[0, :, :POOL_WIDTH], preferred_element_type=F32)
        for gi in range(len(POOL_WINDOWS)):
            pbuf_ref[gi, POOL_HALO + r * sub:POOL_HALO + (r + 1) * sub, :] = (
                p_all[:, gi * POOL_GROUP_DIM:(gi + 1) * POOL_GROUP_DIM])
        u = _gelu_exact(jnp.dot(xn, w_in_ref[0, :, u_cols], preferred_element_type=F32))

        base = r * sub
        n_tmp = 0
        for gi, win in enumerate(POOL_WINDOWS):
            lanes = slice(gi * POOL_GROUP_DIM, (gi + 1) * POOL_GROUP_DIM)

            def read(lo, n, gi=gi):
                return pbuf_ref[gi, base + lo:base + lo + n, :]

            span, first = 1, 0
            while 2 * span < win:
                first += 8
                n = POOL_HALO + sub - first
                tmp = ptmp_ref.at[r, n_tmp]
                n_tmp += 1
                tmp[first:first + n, :] = read(first, n) + read(first - span, n)

                def read(lo, n, tmp=tmp):
                    return tmp[lo:lo + n, :]

                span *= 2
            s = read(POOL_HALO, sub) + read(POOL_HALO - span, sub)
            p = pbuf_ref[gi, base + POOL_HALO:base + POOL_HALO + sub, :]
            if r == 0:
                nh = POOL_WINDOWS[-1]
                t1 = lax.broadcasted_iota(jnp.int32, (nh, 1), 0) + 1
                count = jnp.where(tile_in_seq == 0, jnp.minimum(t1, win), win)
                head = s[:nh, :] * (1.0 / count.astype(F32)) - p[:nh, :]
                ycat_ref[0:nh, lanes] = head.astype(BF16)
                ycat_ref[nh:sub, lanes] = (s[nh:, :] * (1.0 / win) - p[nh:, :]).astype(BF16)
            else:
                ycat_ref[rows, lanes] = (s * (1.0 / win) - p).astype(BF16)

        for hd in range(SGU_HEADS):
            lanes = slice(hd * SGU_HEAD_DIM, (hd + 1) * SGU_HEAD_DIM)
            w_masked = jnp.where(tri, sgu_w_ref[0, hd], jnp.zeros((), BF16))
            bias = sgu_b_ref[0, :, lanes]
            for c in range(sub // CHUNK):
                crows = slice(r * sub + c * CHUNK, r * sub + (c + 1) * CHUNK)
                lrows = slice(c * CHUNK, (c + 1) * CHUNK)
                z = jnp.dot(w_masked, vn[lrows, lanes], preferred_element_type=F32) + bias
                ycat_ref[crows, POOL_WIDTH + hd * SGU_HEAD_DIM:POOL_WIDTH + (hd + 1) * SGU_HEAD_DIM] = (
                    u[lrows, lanes] * z).astype(BF16)

        o_ref[rows, :] = h_ref[rows, :] + jnp.dot(ycat_ref[rows, :], w_out_ref[0], preferred_element_type=F32)

    pbuf_ref[:, 0:POOL_HALO, :] = pbuf_ref[:, ts:, :]


def _mixer(h, g, w_in, sgu_g, sgu_w, sgu_b_full, w_out, *, layer, seq):
    rows = h.shape[0]
    ts = ROW_TILE * MIXER_SUBTILES
    return pl.pallas_call(
        partial(_mixer_kernel, tiles_per_seq=seq // ts),
        out_shape=jax.ShapeDtypeStruct(h.shape, h.dtype),
        grid=(rows // ts,),
        in_specs=[
            pl.BlockSpec((ts, D_MODEL), lambda i: (i, 0)),
            _layer_spec((1, D_MODEL), layer),
            _layer_spec((D_MODEL, D_IN_PROJ), layer),
            _layer_spec((1, SGU_WIDTH), layer),
            _layer_spec((SGU_HEADS, CHUNK, CHUNK), layer),
            _layer_spec((CHUNK, SGU_WIDTH), layer),
            _layer_spec((D_MODEL, D_MODEL), layer),
        ],
        out_specs=pl.BlockSpec((ts, D_MODEL), lambda i: (i, 0)),
        scratch_shapes=[
            pltpu.VMEM((len(POOL_WINDOWS), POOL_HALO + ts, POOL_GROUP_DIM), F32),
            pltpu.VMEM((MIXER_SUBTILES, POOL_LEVEL_BUFFERS, POOL_HALO + ts // MIXER_SUBTILES, POOL_GROUP_DIM), F32),
            pltpu.VMEM((ts, D_MODEL), BF16),
        ],
        compiler_params=pltpu.CompilerParams(
            dimension_semantics=("arbitrary",), vmem_limit_bytes=VMEM_LIMIT_BYTES),
        name="mixer",
    )(h, g, w_in, sgu_g, sgu_w, sgu_b_full, w_out)


def _memkv_kernel(mem_ref, g_ref, wk_ref, wv_ref, wq_ref, wo_ref, wqk_ref, vo_ref):
    memn = _rmsnorm(mem_ref[0], g_ref[0]).astype(BF16)
    k = jnp.dot(memn, wk_ref[0], preferred_element_type=F32).astype(BF16)
    v = jnp.dot(memn, wv_ref[0], preferred_element_type=F32).astype(BF16)
    for hd in range(XATTN_HEADS):
        cols = slice(hd * XATTN_HEAD_DIM, (hd + 1) * XATTN_HEAD_DIM)
        qk = lax.dot_general(wq_ref[0, :, cols], k[:, cols], (((1,), (1,)), ((), ())),
                             preferred_element_type=F32)
        wqk_ref[0, 0, :, hd * N_MEM:(hd + 1) * N_MEM] = (qk * (XATTN_HEAD_DIM ** -0.5)).astype(BF16)
        vo_ref[0, 0, hd * N_MEM:(hd + 1) * N_MEM, :] = jnp.dot(
            v[:, cols], wo_ref[0, cols, :], preferred_element_type=F32).astype(BF16)


def _memkv(mem, mem_g, wk, wv, wq, wo):
    depth = wk.shape[0]
    batch = mem.shape[0]
    w_spec = pl.BlockSpec((1, D_MODEL, D_MODEL), lambda l, b: (l, 0, 0))
    return pl.pallas_call(
        _memkv_kernel,
        out_shape=(jax.ShapeDtypeStruct((depth, batch, D_MODEL, XATTN_HEADS * N_MEM), BF16),
                   jax.ShapeDtypeStruct((depth, batch, XATTN_HEADS * N_MEM, D_MODEL), BF16)),
        grid=(depth, batch),
        in_specs=[
            pl.BlockSpec((1, N_MEM, D_MODEL), lambda l, b: (b, 0, 0)),
            pl.BlockSpec((1, 1, D_MODEL), lambda l, b: (l, 0, 0)),
            w_spec, w_spec, w_spec, w_spec,
        ],
        out_specs=(pl.BlockSpec((1, 1, D_MODEL, XATTN_HEADS * N_MEM), lambda l, b: (l, b, 0, 0)),
                   pl.BlockSpec((1, 1, XATTN_HEADS * N_MEM, D_MODEL), lambda l, b: (l, b, 0, 0))),
        compiler_params=pltpu.CompilerParams(
            dimension_semantics=("arbitrary", "arbitrary"), vmem_limit_bytes=VMEM_LIMIT_BYTES),
        name="memkv",
    )(mem, mem_g, wk, wv, wq, wo)


def _xattn_kernel(h_ref, g_ref, wqk_ref, vo_ref, o_ref, pr_ref):
    sub = h_ref.shape[0] // XATTN_SUBTILES

    def scores(r):
        rows = slice(r * sub, (r + 1) * sub)
        xn = _rmsnorm(h_ref[rows, :], g_ref[0]).astype(BF16)
        s = jnp.dot(xn, wqk_ref[0, 0], preferred_element_type=F32)
        for hd in range(XATTN_HEADS):
            cols = slice(hd * N_MEM, (hd + 1) * N_MEM)
            sh = s[:, cols]
            e = jnp.exp(sh - jnp.max(sh, axis=-1, keepdims=True))
            pr_ref[rows, cols] = (e / jnp.sum(e, axis=-1, keepdims=True)).astype(BF16)

    def attend(r):
        rows = slice(r * sub, (r + 1) * sub)
        o_ref[rows, :] = h_ref[rows, :] + jnp.dot(pr_ref[rows, :], vo_ref[0, 0], preferred_element_type=F32)

    scores(0)
    for r in range(XATTN_SUBTILES):
        if r + 1 < XATTN_SUBTILES:
            scores(r + 1)
        attend(r)


def _xattn(h, g, wqk, vo, *, layer, seq):
    rows = h.shape[0]
    ts = ROW_TILE * XATTN_SUBTILES
    tiles_per_seq = seq // ts
    kv_spec = pl.BlockSpec((1, 1) + wqk.shape[2:], lambda i: (layer, i // tiles_per_seq, 0, 0))
    return pl.pallas_call(
        _xattn_kernel,
        out_shape=jax.ShapeDtypeStruct(h.shape, h.dtype),
        grid=(rows // ts,),
        in_specs=[
            pl.BlockSpec((ts, D_MODEL), lambda i: (i, 0)),
            _layer_spec((1, D_MODEL), layer),
            kv_spec,
            kv_spec,
        ],
        out_specs=pl.BlockSpec((ts, D_MODEL), lambda i: (i, 0)),
        scratch_shapes=[pltpu.VMEM((ts, XATTN_HEADS * N_MEM), BF16)],
        compiler_params=pltpu.CompilerParams(
            dimension_semantics=("arbitrary",), vmem_limit_bytes=VMEM_LIMIT_BYTES),
        name="xattn",
    )(h, g, wqk, vo)


def _ffn_kernel(h_ref, g_ref, w_up_ref, conv_w_ref, conv_b_ref, w_down_ref, fg_ref, o_ref,
                xn_ref, up_ref, act_ref, *, tiles_per_seq, final_norm):
    ts = h_ref.shape[0]
    i = pl.program_id(0)
    slabs_per_chunk = FF_CHUNK // LANES
    val_slab0 = D_FF // LANES
    halves = (slice(0, ts // 2), slice(ts // 2, ts))

    @pl.when(i % tiles_per_seq == 0)
    def _():
        up_ref[:, 0:CARRY_ROWS, :] = jnp.zeros((up_ref.shape[0], CARRY_ROWS, LANES), F32)

    def up_project(first_slab, rows=slice(0, FFN_ROW_TILE)):
        c0 = first_slab * LANES
        up = jnp.dot(xn_ref[rows, :], w_up_ref[0, :, c0:c0 + FF_CHUNK], preferred_element_type=F32)
        for c in range(slabs_per_chunk):
            up_ref[first_slab + c, CARRY_ROWS + rows.start:CARRY_ROWS + rows.stop, :] = (
                up[:, c * LANES:(c + 1) * LANES])

    def conv(slab):
        cols = slice(slab * LANES, (slab + 1) * LANES)
        cur = up_ref[slab, CARRY_ROWS:CARRY_ROWS + ts, :]
        prev1 = up_ref[slab, CARRY_ROWS - 1:CARRY_ROWS - 1 + ts, :]
        prev2 = up_ref[slab, CARRY_ROWS - 2:CARRY_ROWS - 2 + ts, :]
        return (conv_b_ref[0, :, cols] + conv_w_ref[0, 0:1, cols] * prev2 + conv_w_ref[0, 1:2, cols] * prev1
                + conv_w_ref[0, 2:3, cols] * cur)

    for rows in halves:
        xn_ref[rows, :] = _rmsnorm(h_ref[rows, :], g_ref[0]).astype(BF16)
        up_project(0, rows)
        up_project(val_slab0, rows)
    for j in range(D_FF // FF_CHUNK):
        if j > 0:
            up_project(j * slabs_per_chunk)
            up_project(val_slab0 + j * slabs_per_chunk)
        for c in range(slabs_per_chunk):
            slab = j * slabs_per_chunk + c
            gate = conv(slab)
            val = conv(val_slab0 + slab)
            act = gate * (1.0 / (1.0 + jnp.exp(-gate))) * val
            act_ref[:, slab * LANES:(slab + 1) * LANES] = act.astype(BF16)

    up_ref[:, 0:CARRY_ROWS, :] = up_ref[:, ts:ts + CARRY_ROWS, :]
    for rows in halves:
        y = h_ref[rows, :] + jnp.dot(act_ref[rows, :], w_down_ref[0], preferred_element_type=F32)
        if final_norm:
            y = _rmsnorm(y, fg_ref[...])
        o_ref[rows, :] = y


def _ffn(h, g, w_up, conv_w, conv_b, w_down, final_g, *, layer, seq, final_norm):
    rows = h.shape[0]
    ts = FFN_ROW_TILE
    return pl.pallas_call(
        partial(_ffn_kernel, tiles_per_seq=seq // ts, final_norm=final_norm),
        out_shape=jax.ShapeDtypeStruct(h.shape, h.dtype),
        grid=(rows // ts,),
        in_specs=[
            pl.BlockSpec((ts, D_MODEL), lambda i: (i, 0)),
            _layer_spec((1, D_MODEL), layer),
            _layer_spec((D_MODEL, 2 * D_FF), layer),
            _layer_spec((CONV_WIDTH, 2 * D_FF), layer),
            _layer_spec((1, 2 * D_FF), layer),
            _layer_spec((D_FF, D_MODEL), layer),
            pl.BlockSpec((1, D_MODEL), lambda i: (0, 0), pipeline_mode=pl.Buffered(1)),
        ],
        out_specs=pl.BlockSpec((ts, D_MODEL), lambda i: (i, 0)),
        scratch_shapes=[
            pltpu.VMEM((ts, D_MODEL), BF16),
            pltpu.VMEM((2 * D_FF // LANES, CARRY_ROWS + ts, LANES), F32),
            pltpu.VMEM((ts, D_FF), BF16),
        ],
        compiler_params=pltpu.CompilerParams(
            dimension_semantics=("arbitrary",), vmem_limit_bytes=VMEM_LIMIT_BYTES),
        name="conv_ffn",
    )(h, g, w_up, conv_w, conv_b, w_down, final_g)


def _outfold_kernel(pool_w_ref, pool_scale_ref, w_out_ref, o_ref):
    for gi in range(len(POOL_WINDOWS)):
        rows = slice(gi * POOL_GROUP_DIM, (gi + 1) * POOL_GROUP_DIM)
        a = pool_w_ref[0, gi] * pool_scale_ref[0, :, rows]
        o_ref[0, rows, :] = jnp.dot(a, w_out_ref[0, rows, :], preferred_element_type=F32,
                                    precision=lax.Precision.HIGHEST).astype(BF16)
    o_ref[0, POOL_WIDTH:, :] = w_out_ref[0, POOL_WIDTH:, :].astype(BF16)


def _outfold(pool_w, pool_scale, w_out):
    depth = w_out.shape[0]
    return pl.pallas_call(
        _outfold_kernel,
        out_shape=jax.ShapeDtypeStruct(w_out.shape, BF16),
        grid=(depth,),
        in_specs=[
            pl.BlockSpec((1,) + pool_w.shape[1:], lambda l: (l, 0, 0, 0)),
            pl.BlockSpec((1, 1, POOL_WIDTH), lambda l: (l, 0, 0)),
            pl.BlockSpec((1, D_MODEL, D_MODEL), lambda l: (l, 0, 0)),
        ],
        out_specs=pl.BlockSpec((1, D_MODEL, D_MODEL), lambda l: (l, 0, 0)),
        compiler_params=pltpu.CompilerParams(
            dimension_semantics=("arbitrary",), vmem_limit_bytes=VMEM_LIMIT_BYTES),
        name="outfold",
    )(pool_w, pool_scale, w_out)


def kernel(x, mem, norm_mix_g, w_in, pool_w, pool_scale, sgu_g, sgu_w, sgu_b, w_out, norm_xattn_g, mem_norm_g,
           wq, wk, wv, wo, norm_ffn_g, w_up, conv_w, conv_b, w_down, final_norm_g):
    batch, seq, d_model = x.shape
    depth = w_in.shape[0]
    assert d_model == D_MODEL and seq % (ROW_TILE * XATTN_SUBTILES) == 0 and seq % (ROW_TILE * MIXER_SUBTILES) == 0
    assert seq % FFN_ROW_TILE == 0 and ROW_TILE % CHUNK == 0 and D_FF % FF_CHUNK == 0

    def bf(w):
        return w.astype(BF16)

    def row(p):
        return p[:, None, :]

    wqk, vo = _memkv(mem, row(mem_norm_g), bf(wk), bf(wv), bf(wq), bf(wo))
    w_out_folded = _outfold(pool_w, row(pool_scale), w_out)
    sgu_b_full = jnp.repeat(jnp.swapaxes(sgu_b, 1, 2), SGU_HEAD_DIM, axis=2)
    w_in_b, sgu_w_b, w_up_b, w_down_b = bf(w_in), bf(sgu_w), bf(w_up), bf(w_down)

    h = x.reshape(batch * seq, d_model)
    for l in range(depth):
        h = _mixer(h, row(norm_mix_g), w_in_b, row(sgu_g), sgu_w_b, sgu_b_full, w_out_folded, layer=l, seq=seq)
        h = _xattn(h, row(norm_xattn_g), wqk, vo, layer=l, seq=seq)
        h = _ffn(h, row(norm_ffn_g), w_up_b, conv_w, row(conv_b), w_down_b, final_norm_g[None],
                 layer=l, seq=seq, final_norm=(l == depth - 1))
    return h.reshape(batch, seq, d_model)
```

```python
from functools import partial

import jax
import jax.numpy as jnp
from jax import lax
from jax.experimental import pallas as pl
from jax.experimental.pallas import tpu as pltpu

D_MODEL = 1024
N_MEM = 256
POOL_WIDTH = 512
POOL_WINDOWS = (2, 4, 8, 16)
POOL_GROUP_DIM = 128
POOL_LEVEL_BUFFERS = 6
POOL_HALO = 32
SGU_WIDTH = 512
SGU_HEADS = 4
SGU_HEAD_DIM = 128
CHUNK = 128
D_IN_PROJ = POOL_WIDTH + 2 * SGU_WIDTH
XATTN_HEADS = 4
XATTN_HEAD_DIM = 256
D_FF = 2816
CONV_WIDTH = 3
EPS = 1e-6

ROW_TILE = 512
MIXER_SUBTILES = 4
XATTN_SUBTILES = 4
FFN_ROW_TILE = 512
FF_CHUNK = 256
LANES = 128
CARRY_ROWS = 8
VMEM_LIMIT_BYTES = 56 * 1024 * 1024

BF16 = jnp.bfloat16
F32 = jnp.float32


def _rmsnorm(x, g):
    return x * lax.rsqrt(jnp.mean(x * x, axis=-1, keepdims=True) + EPS) * g


def _gelu_exact(x):
    return 0.5 * x * (1.0 + lax.erf(x * (0.5 ** 0.5)))


def _layer_spec(tail, layer):
    return pl.BlockSpec((1,) + tail, lambda i: (layer,) + (0,) * len(tail), pipeline_mode=pl.Buffered(1))


def _mixer_kernel(h_ref, g_ref, w_in_ref, sgu_g_ref, sgu_w_ref, sgu_b_ref, w_out_ref, o_ref,
                  pbuf_ref, ptmp_ref, ycat_ref, *, tiles_per_seq):
    ts = h_ref.shape[0]
    sub = ts // MIXER_SUBTILES
    i = pl.program_id(0)
    tile_in_seq = i % tiles_per_seq
    u_cols = slice(POOL_WIDTH, POOL_WIDTH + SGU_WIDTH)
    v_cols = slice(POOL_WIDTH + SGU_WIDTH, D_IN_PROJ)
    tri = (lax.broadcasted_iota(jnp.int32, (CHUNK, CHUNK), 0)
           >= lax.broadcasted_iota(jnp.int32, (CHUNK, CHUNK), 1))

    @pl.when(tile_in_seq == 0)
    def _():
        pbuf_ref[:, 0:POOL_HALO, :] = jnp.zeros((len(POOL_WINDOWS), POOL_HALO, POOL_GROUP_DIM), F32)

    for r in range(MIXER_SUBTILES):
        rows = slice(r * sub, (r + 1) * sub)
        xn = _rmsnorm(h_ref[rows, :], g_ref[0]).astype(BF16)

        v = _gelu_exact(jnp.dot(xn, w_in_ref[0, :, v_cols], preferred_element_type=F32))
        mu = jnp.mean(v, axis=-1, keepdims=True)
        vc = v - mu
        vn = (vc * lax.rsqrt(jnp.mean(vc * vc, axis=-1, keepdims=True) + EPS) * sgu_g_ref[0]).astype(BF16)
        p_all = jnp.dot(xn, w_in_ref[0, :, :POOL_WIDTH], preferred_element_type=F32)
        for gi in range(len(POOL_WINDOWS)):
            pbuf_ref[gi, POOL_HALO + r * sub:POOL_HALO + (r + 1) * sub, :] = (
                p_all[:, gi * POOL_GROUP_DIM:(gi + 1) * POOL_GROUP_DIM])
        u = _gelu_exact(jnp.dot(xn, w_in_ref[0, :, u_cols], preferred_element_type=F32))

        base = r * sub
        n_tmp = 0
        for gi, win in enumerate(POOL_WINDOWS):
            lanes = slice(gi * POOL_GROUP_DIM, (gi + 1) * POOL_GROUP_DIM)

            def read(lo, n, gi=gi):
                return pbuf_ref[gi, base + lo:base + lo + n, :]

            span, first = 1, 0
            while 2 * span < win:
                first += 8
                n = POOL_HALO + sub - first
                tmp = ptmp_ref.at[r, n_tmp]
                n_tmp += 1
                tmp[first:first + n, :] = read(first, n) + read(first - span, n)

                def read(lo, n, tmp=tmp):
                    return tmp[lo:lo + n, :]

                span *= 2
            s = read(POOL_HALO, sub) + read(POOL_HALO - span, sub)
            p = pbuf_ref[gi, base + POOL_HALO:base + POOL_HALO + sub, :]
            if r == 0:
                nh = POOL_WINDOWS[-1]
                t1 = lax.broadcasted_iota(jnp.int32, (nh, 1), 0) + 1
                count = jnp.where(tile_in_seq == 0, jnp.minimum(t1, win), win)
                head = s[:nh, :] * (1.0 / count.astype(F32)) - p[:nh, :]
                ycat_ref[0:nh, lanes] = head.astype(BF16)
                ycat_ref[nh:sub, lanes] = (s[nh:, :] * (1.0 / win) - p[nh:, :]).astype(BF16)
            else:
                ycat_ref[rows, lanes] = (s * (1.0 / win) - p).astype(BF16)

        for hd in range(SGU_HEADS):
            lanes = slice(hd * SGU_HEAD_DIM, (hd + 1) * SGU_HEAD_DIM)
            w_masked = jnp.where(tri, sgu_w_ref[0, hd], jnp.zeros((), BF16))
            bias = sgu_b_ref[0, :, lanes]
            for c in range(sub // CHUNK):
                crows = slice(r * sub + c * CHUNK, r * sub + (c + 1) * CHUNK)
                lrows = slice(c * CHUNK, (c + 1) * CHUNK)
                z = jnp.dot(w_masked, vn[lrows, lanes], preferred_element_type=F32) + bias
                ycat_ref[crows, POOL_WIDTH + hd * SGU_HEAD_DIM:POOL_WIDTH + (hd + 1) * SGU_HEAD_DIM] = (
                    u[lrows, lanes] * z).astype(BF16)

        o_ref[rows, :] = h_ref[rows, :] + jnp.dot(ycat_ref[rows, :], w_out_ref[0], preferred_element_type=F32)

    pbuf_ref[:, 0:POOL_HALO, :] = pbuf_ref[:, ts:, :]


def _mixer(h, g, w_in, sgu_g, sgu_w, sgu_b_full, w_out, *, layer, seq):
    rows = h.shape[0]
    ts = ROW_TILE * MIXER_SUBTILES
    return pl.pallas_call(
        partial(_mixer_kernel, tiles_per_seq=seq // ts),
        out_shape=jax.ShapeDtypeStruct(h.shape, h.dtype),
        grid=(rows // ts,),
        in_specs=[
            pl.BlockSpec((ts, D_MODEL), lambda i: (i, 0)),
            _layer_spec((1, D_MODEL), layer),
            _layer_spec((D_MODEL, D_IN_PROJ), layer),
            _layer_spec((1, SGU_WIDTH), layer),
            _layer_spec((SGU_HEADS, CHUNK, CHUNK), layer),
            _layer_spec((CHUNK, SGU_WIDTH), layer),
            _layer_spec((D_MODEL, D_MODEL), layer),
        ],
        out_specs=pl.BlockSpec((ts, D_MODEL), lambda i: (i, 0)),
        scratch_shapes=[
            pltpu.VMEM((len(POOL_WINDOWS), POOL_HALO + ts, POOL_GROUP_DIM), F32),
            pltpu.VMEM((MIXER_SUBTILES, POOL_LEVEL_BUFFERS, POOL_HALO + ts // MIXER_SUBTILES, POOL_GROUP_DIM), F32),
            pltpu.VMEM((ts, D_MODEL), BF16),
        ],
        compiler_params=pltpu.CompilerParams(
            dimension_semantics=("arbitrary",), vmem_limit_bytes=VMEM_LIMIT_BYTES),
        name="mixer",
    )(h, g, w_in, sgu_g, sgu_w, sgu_b_full, w_out)


def _memkv_kernel(mem_ref, g_ref, wk_ref, wv_ref, wq_ref, wo_ref, wqk_ref, vo_ref):
    memn = _rmsnorm(mem_ref[0], g_ref[0]).astype(BF16)
    k = jnp.dot(memn, wk_ref[0], preferred_element_type=F32).astype(BF16)
    v = jnp.dot(memn, wv_ref[0], preferred_element_type=F32).astype(BF16)
    for hd in range(XATTN_HEADS):
        cols = slice(hd * XATTN_HEAD_DIM, (hd + 1) * XATTN_HEAD_DIM)
        qk = lax.dot_general(wq_ref[0, :, cols], k[:, cols], (((1,), (1,)), ((), ())),
                             preferred_element_type=F32)
        wqk_ref[0, 0, :, hd * N_MEM:(hd + 1) * N_MEM] = (qk * (XATTN_HEAD_DIM ** -0.5)).astype(BF16)
        vo_ref[0, 0, hd * N_MEM:(hd + 1) * N_MEM, :] = jnp.dot(
            v[:, cols], wo_ref[0, cols, :], preferred_element_type=F32).astype(BF16)


def _memkv(mem, mem_g, wk, wv, wq, wo):
    depth = wk.shape[0]
    batch = mem.shape[0]
    w_spec = pl.BlockSpec((1, D_MODEL, D_MODEL), lambda l, b: (l, 0, 0))
    return pl.pallas_call(
        _memkv_kernel,
        out_shape=(jax.ShapeDtypeStruct((depth, batch, D_MODEL, XATTN_HEADS * N_MEM), BF16),
                   jax.ShapeDtypeStruct((depth, batch, XATTN_HEADS * N_MEM, D_MODEL), BF16)),
        grid=(depth, batch),
        in_specs=[
            pl.BlockSpec((1, N_MEM, D_MODEL), lambda l, b: (b, 0, 0)),
            pl.BlockSpec((1, 1, D_MODEL), lambda l, b: (l, 0, 0)),
            w_spec, w_spec, w_spec, w_spec,
        ],
        out_specs=(pl.BlockSpec((1, 1, D_MODEL, XATTN_HEADS * N_MEM), lambda l, b: (l, b, 0, 0)),
                   pl.BlockSpec((1, 1, XATTN_HEADS * N_MEM, D_MODEL), lambda l, b: (l, b, 0, 0))),
        compiler_params=pltpu.CompilerParams(
            dimension_semantics=("arbitrary", "arbitrary"), vmem_limit_bytes=VMEM_LIMIT_BYTES),
        name="memkv",
    )(mem, mem_g, wk, wv, wq, wo)


def _xattn_kernel(h_ref, g_ref, wqk_ref, vo_ref, o_ref, pr_ref):
    sub = h_ref.shape[0] // XATTN_SUBTILES

    def scores(r):
        rows = slice(r * sub, (r + 1) * sub)
        xn = _rmsnorm(h_ref[rows, :], g_ref[0]).astype(BF16)
        s = jnp.dot(xn, wqk_ref[0, 0], preferred_element_type=F32)
        for hd in range(XATTN_HEADS):
            cols = slice(hd * N_MEM, (hd + 1) * N_MEM)
            sh = s[:, cols]
            e = jnp.exp(sh - jnp.max(sh, axis=-1, keepdims=True))
            pr_ref[rows, cols] = (e / jnp.sum(e, axis=-1, keepdims=True)).astype(BF16)

    def attend(r):
        rows = slice(r * sub, (r + 1) * sub)
        o_ref[rows, :] = h_ref[rows, :] + jnp.dot(pr_ref[rows, :], vo_ref[0, 0], preferred_element_type=F32)

    scores(0)
    for r in range(XATTN_SUBTILES):
        if r + 1 < XATTN_SUBTILES:
            scores(r + 1)
        attend(r)


def _xattn(h, g, wqk, vo, *, layer, seq):
    rows = h.shape[0]
    ts = ROW_TILE * XATTN_SUBTILES
    tiles_per_seq = seq // ts
    kv_spec = pl.BlockSpec((1, 1) + wqk.shape[2:], lambda i: (layer, i // tiles_per_seq, 0, 0))
    return pl.pallas_call(
        _xattn_kernel,
        out_shape=jax.ShapeDtypeStruct(h.shape, h.dtype),
        grid=(rows // ts,),
        in_specs=[
            pl.BlockSpec((ts, D_MODEL), lambda i: (i, 0)),
            _layer_spec((1, D_MODEL), layer),
            kv_spec,
            kv_spec,
        ],
        out_specs=pl.BlockSpec((ts, D_MODEL), lambda i: (i, 0)),
        scratch_shapes=[pltpu.VMEM((ts, XATTN_HEADS * N_MEM), BF16)],
        compiler_params=pltpu.CompilerParams(
            dimension_semantics=("arbitrary",), vmem_limit_bytes=VMEM_LIMIT_BYTES),
        name="xattn",
    )(h, g, wqk, vo)


def _ffn_kernel(h_ref, g_ref, w_up_ref, conv_w_ref, conv_b_ref, w_down_ref, fg_ref, o_ref,
                xn_ref, up_ref, act_ref, *, tiles_per_seq, final_norm):
    ts = h_ref.shape[0]
    i = pl.program_id(0)
    slabs_per_chunk = FF_CHUNK // LANES
    val_slab0 = D_FF // LANES
    halves = (slice(0, ts // 2), slice(ts // 2, ts))

    @pl.when(i % tiles_per_seq == 0)
    def _():
        up_ref[:, 0:CARRY_ROWS, :] = jnp.zeros((up_ref.shape[0], CARRY_ROWS, LANES), F32)

    def up_project(first_slab, rows=slice(0, FFN_ROW_TILE)):
        c0 = first_slab * LANES
        up = jnp.dot(xn_ref[rows, :], w_up_ref[0, :, c0:c0 + FF_CHUNK], preferred_element_type=F32)
        for c in range(slabs_per_chunk):
            up_ref[first_slab + c, CARRY_ROWS + rows.start:CARRY_ROWS + rows.stop, :] = (
                up[:, c * LANES:(c + 1) * LANES])

    def conv(slab):
        cols = slice(slab * LANES, (slab + 1) * LANES)
        cur = up_ref[slab, CARRY_ROWS:CARRY_ROWS + ts, :]
        prev1 = up_ref[slab, CARRY_ROWS - 1:CARRY_ROWS - 1 + ts, :]
        prev2 = up_ref[slab, CARRY_ROWS - 2:CARRY_ROWS - 2 + ts, :]
        return (conv_b_ref[0, :, cols] + conv_w_ref[0, 0:1, cols] * prev2 + conv_w_ref[0, 1:2, cols] * prev1
                + conv_w_ref[0, 2:3, cols] * cur)

    for rows in halves:
        xn_ref[rows, :] = _rmsnorm(h_ref[rows, :], g_ref[0]).astype(BF16)
        up_project(0, rows)
        up_project(val_slab0, rows)
    for j in range(D_FF // FF_CHUNK):
        if j > 0:
            up_project(j * slabs_per_chunk)
            up_project(val_slab0 + j * slabs_per_chunk)
        for c in range(slabs_per_chunk):
            slab = j * slabs_per_chunk + c
            gate = conv(slab)
            val = conv(val_slab0 + slab)
            act = gate * (1.0 / (1.0 + jnp.exp(-gate))) * val
            act_ref[:, slab * LANES:(slab + 1) * LANES] = act.astype(BF16)

    up_ref[:, 0:CARRY_ROWS, :] = up_ref[:, ts:ts + CARRY_ROWS, :]
    for rows in halves:
        y = h_ref[rows, :] + jnp.dot(act_ref[rows, :], w_down_ref[0], preferred_element_type=F32)
        if final_norm:
            y = _rmsnorm(y, fg_ref[...])
        o_ref[rows, :] = y


def _ffn(h, g, w_up, conv_w, conv_b, w_down, final_g, *, layer, seq, final_norm):
    rows = h.shape[0]
    ts = FFN_ROW_TILE
    return pl.pallas_call(
        partial(_ffn_kernel, tiles_per_seq=seq // ts, final_norm=final_norm),
        out_shape=jax.ShapeDtypeStruct(h.shape, h.dtype),
        grid=(rows // ts,),
        in_specs=[
            pl.BlockSpec((ts, D_MODEL), lambda i: (i, 0)),
            _layer_spec((1, D_MODEL), layer),
            _layer_spec((D_MODEL, 2 * D_FF), layer),
            _layer_spec((CONV_WIDTH, 2 * D_FF), layer),
            _layer_spec((1, 2 * D_FF), layer),
            _layer_spec((D_FF, D_MODEL), layer),
            pl.BlockSpec((1, D_MODEL), lambda i: (0, 0), pipeline_mode=pl.Buffered(1)),
        ],
        out_specs=pl.BlockSpec((ts, D_MODEL), lambda i: (i, 0)),
        scratch_shapes=[
            pltpu.VMEM((ts, D_MODEL), BF16),
            pltpu.VMEM((2 * D_FF // LANES, CARRY_ROWS + ts, LANES), F32),
            pltpu.VMEM((ts, D_FF), BF16),
        ],
        compiler_params=pltpu.CompilerParams(
            dimension_semantics=("arbitrary",), vmem_limit_bytes=VMEM_LIMIT_BYTES),
        name="conv_ffn",
    )(h, g, w_up, conv_w, conv_b, w_down, final_g)


def _outfold_kernel(pool_w_ref, pool_scale_ref, w_out_ref, o_ref):
    for gi in range(len(POOL_WINDOWS)):
        rows = slice(gi * POOL_GROUP_DIM, (gi + 1) * POOL_GROUP_DIM)
        a = pool_w_ref[0, gi] * pool_scale_ref[0, :, rows]
        o_ref[0, rows, :] = jnp.dot(a, w_out_ref[0, rows, :], preferred_element_type=F32,
                                    precision=lax.Precision.HIGHEST).astype(BF16)
    o_ref[0, POOL_WIDTH:, :] = w_out_ref[0, POOL_WIDTH:, :].astype(BF16)


def _outfold(pool_w, pool_scale, w_out):
    depth = w_out.shape[0]
    return pl.pallas_call(
        _outfold_kernel,
        out_shape=jax.ShapeDtypeStruct(w_out.shape, BF16),
        grid=(depth,),
        in_specs=[
            pl.BlockSpec((1,) + pool_w.shape[1:], lambda l: (l, 0, 0, 0)),
            pl.BlockSpec((1, 1, POOL_WIDTH), lambda l: (l, 0, 0)),
            pl.BlockSpec((1, D_MODEL, D_MODEL), lambda l: (l, 0, 0)),
        ],
        out_specs=pl.BlockSpec((1, D_MODEL, D_MODEL), lambda l: (l, 0, 0)),
        compiler_params=pltpu.CompilerParams(
            dimension_semantics=("arbitrary",), vmem_limit_bytes=VMEM_LIMIT_BYTES),
        name="outfold",
    )(pool_w, pool_scale, w_out)


def kernel(x, mem, norm_mix_g, w_in, pool_w, pool_scale, sgu_g, sgu_w, sgu_b, w_out, norm_xattn_g, mem_norm_g,
           wq, wk, wv, wo, norm_ffn_g, w_up, conv_w, conv_b, w_down, final_norm_g):
    batch, seq, d_model = x.shape
    depth = w_in.shape[0]
    assert d_model == D_MODEL and seq % (ROW_TILE * XATTN_SUBTILES) == 0 and seq % (ROW_TILE * MIXER_SUBTILES) == 0
    assert seq % FFN_ROW_TILE == 0 and ROW_TILE % CHUNK == 0 and D_FF % FF_CHUNK == 0

    def bf(w):
        return w.astype(BF16)

    def row(p):
        return p[:, None, :]

    wqk, vo = _memkv(mem, row(mem_norm_g), bf(wk), bf(wv), bf(wq), bf(wo))
    w_out_folded = _outfold(pool_w, row(pool_scale), w_out)
    sgu_b_full = jnp.repeat(jnp.swapaxes(sgu_b, 1, 2), SGU_HEAD_DIM, axis=2)
    w_in_b, sgu_w_b, w_up_b, w_down_b = bf(w_in), bf(sgu_w), bf(w_up), bf(w_down)

    h = x.reshape(batch * seq, d_model)
    for l in range(depth):
        h = _mixer(h, row(norm_mix_g), w_in_b, row(sgu_g), sgu_w_b, sgu_b_full, w_out_folded, layer=l, seq=seq)
        h = _xattn(h, row(norm_xattn_g), wqk, vo, layer=l, seq=seq)
        h = _ffn(h, row(norm_ffn_g), w_up_b, conv_w, row(conv_b), w_down_b, final_norm_g[None],
                 layer=l, seq=seq, final_norm=(l == depth - 1))
    return h.reshape(batch, seq, d_model)
```
